```python
import jax, jax.numpy as jnp
from jax import lax
import numpy as np

D_MODEL = 1024
BATCH = 8
SEQ = 4096
DEPTH = 2

CHUNK = 64
LN_EPS = 1e-5
GMLP_HEADS = 4
GMLP_WIDTH = D_MODEL
GMLP_HEAD_DIM = GMLP_WIDTH // GMLP_HEADS
GMLP_BLOCK = 128
POOL_WINDOWS = (2, 4, 8, 16)
POOL_GROUPS = len(POOL_WINDOWS)
POOL_WIDTH = D_MODEL
POOL_GROUP_DIM = POOL_WIDTH // POOL_GROUPS
EVEN_IN = 3 * GMLP_WIDTH + 2 * POOL_WIDTH
EVEN_MIX = GMLP_WIDTH + POOL_WIDTH
MLA_HEADS = 16
MLA_NOPE = 128
MLA_ROPE = 64
MLA_V = 128
MLA_Q_RANK = 256
MLA_KV_RANK = 128
MLA_WIDTH = MLA_HEADS * MLA_V
ODD_IN = MLA_Q_RANK + MLA_KV_RANK + MLA_ROPE + MLA_WIDTH
ROPE_THETA = 10000.0
Q_BLOCK = 128
ATTN_SCALE = (MLA_NOPE + MLA_ROPE) ** -0.5
DEEPNORM_ALPHA = (2.0 * DEPTH) ** 0.25
DEEPNORM_BETA = (8.0 * DEPTH) ** -0.25
N_EVEN = (DEPTH + 1) // 2
N_ODD = DEPTH // 2

kernel_name = "hybrid_gmlp_pool_mla_deepnorm_adaln"


def layer_norm(x, g, b):
    xf = x.astype(jnp.float32)
    mu = jnp.mean(xf, axis=-1, keepdims=True)
    var = jnp.mean(jnp.square(xf - mu), axis=-1, keepdims=True)
    return ((xf - mu) * lax.rsqrt(var + LN_EPS) * g + b).astype(x.dtype)


def rms_norm(x, g):
    xf = x.astype(jnp.float32)
    ms = jnp.mean(jnp.square(xf), axis=-1, keepdims=True)
    return (xf * lax.rsqrt(ms + LN_EPS) * g).astype(x.dtype)


def rope_cos_sin(positions):
    inv = 1.0 / (ROPE_THETA ** (jnp.arange(0, MLA_ROPE, 2, dtype=jnp.float32) / MLA_ROPE))
    ang = positions.astype(jnp.float32)[..., None] * inv
    return jnp.cos(ang), jnp.sin(ang)


def apply_rope(x, cos, sin):
    half = x.shape[-1] // 2
    x1 = x[..., :half].astype(jnp.float32)
    x2 = x[..., half:].astype(jnp.float32)
    return jnp.concatenate([x1 * cos - x2 * sin, x2 * cos + x1 * sin], axis=-1).astype(x.dtype)


def gmlp_spatial_unit(u, v, norm_g, norm_b, ws, bs):
    B, S, _ = u.shape
    nb = S // GMLP_BLOCK
    v = layer_norm(v.reshape(B, S, GMLP_HEADS, GMLP_HEAD_DIM), norm_g, norm_b)
    v = v.reshape(B, nb, GMLP_BLOCK, GMLP_HEADS, GMLP_HEAD_DIM)
    pos_chunk = jnp.arange(GMLP_BLOCK) // CHUNK
    mask = pos_chunk[None, :] <= pos_chunk[:, None]
    w = jnp.where(mask[None], ws, jnp.zeros_like(ws))
    sv = jnp.einsum('hts,bnshd->bnthd', w, v) + bs.T[:, :, None]
    return u * sv.reshape(B, S, GMLP_WIDTH)


def multiscale_pool(xb, pool_w, pool_b, pool_scale):
    B, S, _ = xb.shape
    xg = xb.reshape(B, S, POOL_GROUPS, POOL_GROUP_DIM).astype(jnp.float32)
    cs = jnp.cumsum(xg, axis=1)
    t = jnp.arange(S)
    means = []
    for g, win in enumerate(POOL_WINDOWS):
        csg = cs[:, :, g]
        lagged = jnp.concatenate([jnp.zeros((B, win, POOL_GROUP_DIM), csg.dtype), csg[:, :S - win]], axis=1)
        cnt = jnp.minimum(t + 1, win).astype(jnp.float32)
        means.append((csg - lagged) / cnt[None, :, None])
    pooled = jnp.stack(means, axis=2) - xg
    y = jnp.einsum('bsgd,gde->bsge', pooled.astype(xb.dtype), pool_w).reshape(B, S, POOL_WIDTH)
    return (y + pool_b) * pool_scale


def even_mixer(h, w_in, gmlp_norm_g, gmlp_norm_b, gmlp_ws, gmlp_bs, pool_w, pool_b, pool_scale, w_out):
    proj = h @ w_in
    u, v, z_a, x_b, z_b = jnp.split(proj, [GMLP_WIDTH, 2 * GMLP_WIDTH, 3 * GMLP_WIDTH,
                                           3 * GMLP_WIDTH + POOL_WIDTH], axis=-1)
    a = gmlp_spatial_unit(u, v, gmlp_norm_g, gmlp_norm_b, gmlp_ws, gmlp_bs) * jax.nn.silu(z_a)
    b = multiscale_pool(x_b, pool_w, pool_b, pool_scale) * jax.nn.silu(z_b)
    return jnp.concatenate([a, b], axis=-1) @ w_out


def mla_mixer(h, positions, w_in, q_norm_g, kv_norm_g, w_uq, w_uk, w_uv, w_out):
    B, S, _ = h.shape
    proj = h @ w_in
    q_c, kv_c, k_r, z = jnp.split(proj, [MLA_Q_RANK, MLA_Q_RANK + MLA_KV_RANK,
                                         MLA_Q_RANK + MLA_KV_RANK + MLA_ROPE], axis=-1)
    q_c = rms_norm(q_c, q_norm_g)
    kv_c = rms_norm(kv_c, kv_norm_g)
    q = jnp.einsum('bsr,rhd->bshd', q_c, w_uq)
    q_nope, q_rope = q[..., :MLA_NOPE], q[..., MLA_NOPE:]
    cos, sin = rope_cos_sin(positions)
    q_rope = apply_rope(q_rope, cos[:, :, None, :], sin[:, :, None, :])
    k_rope = apply_rope(k_r, cos, sin)
    q_lat = jnp.einsum('bshd,rhd->bshr', q_nope, w_uk)
    nb = S // Q_BLOCK
    q_lat_b = q_lat.reshape(B, nb, Q_BLOCK, MLA_HEADS, MLA_KV_RANK).transpose(1, 0, 2, 3, 4)
    q_rope_b = q_rope.reshape(B, nb, Q_BLOCK, MLA_HEADS, MLA_ROPE).transpose(1, 0, 2, 3, 4)
    key_chunk = jnp.arange(S) // CHUNK

    def attend_block(args):
        ql, qr, i = args
        s = jnp.einsum('bqhr,bkr->bhqk', ql, kv_c) + jnp.einsum('bqhd,bkd->bhqk', qr, k_rope)
        s = s.astype(jnp.float32) * ATTN_SCALE
        q_chunk = (i * Q_BLOCK + jnp.arange(Q_BLOCK)) // CHUNK
        mask = key_chunk[None, :] <= q_chunk[:, None]
        p = jax.nn.softmax(jnp.where(mask, s, -jnp.inf), axis=-1).astype(kv_c.dtype)
        return jnp.einsum('bhqk,bkr->bqhr', p, kv_c)

    o_lat = lax.map(attend_block, (q_lat_b, q_rope_b, jnp.arange(nb)))
    o_lat = o_lat.transpose(1, 0, 2, 3, 4).reshape(B, S, MLA_HEADS, MLA_KV_RANK)
    o = jnp.einsum('bshr,rhd->bshd', o_lat, w_uv).reshape(B, S, MLA_WIDTH)
    return (o * jax.nn.silu(z)) @ w_out


def setup_inputs(seed: int = 0) -> dict:
    key = jax.random.key(seed)
    ks = jax.random.split(key, 24)
    f32 = jnp.float32

    def nrm(k, shape, s):
        return s * jax.random.normal(k, shape, f32)

    x = nrm(ks[0], (BATCH, SEQ, D_MODEL), 1.0)
    c = nrm(ks[1], (BATCH, D_MODEL), 1.0)
    offs = jax.random.randint(ks[2], (BATCH, 1), 0, 4096, dtype=jnp.int32)
    positions = offs + jnp.arange(SEQ, dtype=jnp.int32)[None, :]
    ada_w = nrm(ks[3], (DEPTH, D_MODEL, 3 * D_MODEL), 0.1 * D_MODEL ** -0.5)
    ada_b = nrm(ks[4], (DEPTH, 3 * D_MODEL), 0.01)
    ln_g = 1.0 + nrm(ks[5], (DEPTH, D_MODEL), 0.02)
    ln_b = nrm(ks[6], (DEPTH, D_MODEL), 0.02)
    e_w_in = nrm(ks[7], (N_EVEN, D_MODEL, EVEN_IN), D_MODEL ** -0.5)
    gmlp_norm_g = 1.0 + nrm(ks[8], (N_EVEN, GMLP_HEAD_DIM), 0.02)
    gmlp_norm_b = nrm(ks[9], (N_EVEN, GMLP_HEAD_DIM), 0.02)
    gmlp_ws = nrm(ks[10], (N_EVEN, GMLP_HEADS, GMLP_BLOCK, GMLP_BLOCK), 0.5 * GMLP_BLOCK ** -0.5)
    gmlp_bs = 1.0 + nrm(ks[11], (N_EVEN, GMLP_HEADS, GMLP_BLOCK), 0.02)
    pool_w = nrm(ks[12], (N_EVEN, POOL_GROUPS, POOL_GROUP_DIM, POOL_GROUP_DIM), POOL_GROUP_DIM ** -0.5)
    pool_b = nrm(ks[13], (N_EVEN, POOL_WIDTH), 0.01)
    pool_scale = 1.0 + nrm(ks[14], (N_EVEN, POOL_WIDTH), 0.1)
    e_w_out = nrm(ks[15], (N_EVEN, EVEN_MIX, D_MODEL), DEEPNORM_BETA * EVEN_MIX ** -0.5)
    o_w_in = nrm(ks[16], (N_ODD, D_MODEL, ODD_IN), D_MODEL ** -0.5)
    mla_q_norm_g = 1.0 + nrm(ks[17], (N_ODD, MLA_Q_RANK), 0.02)
    mla_kv_norm_g = 1.0 + nrm(ks[18], (N_ODD, MLA_KV_RANK), 0.02)
    mla_w_uq = nrm(ks[19], (N_ODD, MLA_Q_RANK, MLA_HEADS, MLA_NOPE + MLA_ROPE), MLA_Q_RANK ** -0.5)
    mla_w_uk = nrm(ks[20], (N_ODD, MLA_KV_RANK, MLA_HEADS, MLA_NOPE), MLA_KV_RANK ** -0.5)
    mla_w_uv = nrm(ks[21], (N_ODD, MLA_KV_RANK, MLA_HEADS, MLA_V), MLA_KV_RANK ** -0.5)
    o_w_out = nrm(ks[22], (N_ODD, MLA_WIDTH, D_MODEL), DEEPNORM_BETA * MLA_WIDTH ** -0.5)
    return {"x": x, "c": c, "positions": positions, "ada_w": ada_w, "ada_b": ada_b,
            "ln_g": ln_g, "ln_b": ln_b, "e_w_in": e_w_in, "gmlp_norm_g": gmlp_norm_g,
            "gmlp_norm_b": gmlp_norm_b, "gmlp_ws": gmlp_ws, "gmlp_bs": gmlp_bs,
            "pool_w": pool_w, "pool_b": pool_b, "pool_scale": pool_scale, "e_w_out": e_w_out,
            "o_w_in": o_w_in, "mla_q_norm_g": mla_q_norm_g, "mla_kv_norm_g": mla_kv_norm_g,
            "mla_w_uq": mla_w_uq, "mla_w_uk": mla_w_uk, "mla_w_uv": mla_w_uv, "o_w_out": o_w_out}


def reference(x, c, positions, ada_w, ada_b, ln_g, ln_b, e_w_in, gmlp_norm_g, gmlp_norm_b,
              gmlp_ws, gmlp_bs, pool_w, pool_b, pool_scale, e_w_out, o_w_in, mla_q_norm_g,
              mla_kv_norm_g, mla_w_uq, mla_w_uk, mla_w_uv, o_w_out):
    cond = jax.nn.silu(c)
    for l in range(DEPTH):
        mod = cond @ ada_w[l] + ada_b[l]
        shift, scale, gate = jnp.split(mod, 3, axis=-1)
        h = x * (1.0 + scale[:, None, :]) + shift[:, None, :]
        if l % 2 == 0:
            e = l // 2
            y = even_mixer(h, e_w_in[e], gmlp_norm_g[e], gmlp_norm_b[e], gmlp_ws[e], gmlp_bs[e],
                           pool_w[e], pool_b[e], pool_scale[e], e_w_out[e])
        else:
            o = l // 2
            y = mla_mixer(h, positions, o_w_in[o], mla_q_norm_g[o], mla_kv_norm_g[o],
                          mla_w_uq[o], mla_w_uk[o], mla_w_uv[o], o_w_out[o])
        x = layer_norm(DEEPNORM_ALPHA * x + (1.0 + gate[:, None, :]) * y, ln_g[l], ln_b[l])
    return x
```

```python
import functools
import math

import numpy as np
import jax
import jax.numpy as jnp
from jax import lax
from jax.experimental import pallas as pl
from jax.experimental.pallas import tpu as pltpu

F32 = jnp.float32
BF16 = jnp.bfloat16

CHUNK = 64
LN_EPS = 1e-5
GMLP_HEADS = 4
GMLP_BLOCK = 128
POOL_WINDOWS = (2, 4, 8, 16)
MLA_HEADS = 16
MLA_NOPE = 128
MLA_ROPE = 64
MLA_V = 128
MLA_Q_RANK = 256
MLA_KV_RANK = 128
ROPE_THETA = 10000.0
ATTN_SCALE = (MLA_NOPE + MLA_ROPE) ** -0.5
DEPTH = 2
DEEPNORM_ALPHA = (2.0 * DEPTH) ** 0.25

V7X_LANES = 128
V7X_VMEM_BYTES = 64 * 1024 * 1024
VMEM_LIMIT_BYTES = V7X_VMEM_BYTES - 8 * 1024 * 1024

POOL_HALO = max(POOL_WINDOWS)
HEAD_LANES = 2 * V7X_LANES
MASK_VALUE = -1e30
LOG2E = math.log2(math.e)

L0_TILE = 256
Q_TILE = 256
K_TILE = 256


def _silu(v):
    return v * jax.nn.sigmoid(v)


def _layer_norm(v, g, b):
    mu = jnp.mean(v, axis=-1, keepdims=True)
    d = v - mu
    var = jnp.mean(d * d, axis=-1, keepdims=True)
    return d * lax.rsqrt(var + LN_EPS) * g + b


def _resident(shape):
    nd = len(shape)
    return pl.BlockSpec(shape, lambda *_: (0,) * nd)


def _mod_kernel(c_ref, w_ref, b_ref, o_ref):
    cond = _silu(c_ref[...]).astype(BF16)
    o_ref[0] = jnp.dot(cond, w_ref[0].astype(BF16), preferred_element_type=F32) + b_ref[0]


def _modulation(c, ada_w, ada_b):
    depth, d, n = ada_w.shape
    b = c.shape[0]
    tn = d
    return pl.pallas_call(
        _mod_kernel,
        grid=(depth, n // tn),
        in_specs=[
            pl.BlockSpec((b, d), lambda l, j: (0, 0)),
            pl.BlockSpec((1, d, tn), lambda l, j: (l, 0, j)),
            pl.BlockSpec((1, 1, tn), lambda l, j: (l, 0, j)),
        ],
        out_specs=pl.BlockSpec((1, b, tn), lambda l, j: (l, 0, j)),
        out_shape=jax.ShapeDtypeStruct((depth, b, n), F32),
        compiler_params=pltpu.CompilerParams(
            dimension_semantics=("arbitrary", "arbitrary"), vmem_limit_bytes=VMEM_LIMIT_BYTES),
        name="adaln_mod",
    )(c, ada_w, ada_b.reshape(depth, 1, n))


def _l0_kernel(x_ref, shift_ref, scale_ref, gate_ref, w_in_ref, ng_ref, nb_ref, ws_ref, bsf_ref,
               pw_ref, pb_ref, ps_ref, cnt_ref, w_out_ref, lng_ref, lnb_ref,
               o_ref, xbuf_ref, mix_ref):
    t, d = x_ref.shape[1], x_ref.shape[2]
    width = d
    hd = width // GMLP_HEADS
    gd = width // len(POOL_WINDOWS)
    s_idx = pl.program_id(1)

    x = x_ref[0]
    h = (x * (1.0 + scale_ref[0]) + shift_ref[0]).astype(BF16)

    def proj(i):
        return jnp.dot(h, w_in_ref[:, i * width:(i + 1) * width], preferred_element_type=F32)

    u = proj(0)
    v = proj(1)
    za = proj(2)
    r = lax.broadcasted_iota(jnp.int32, (GMLP_BLOCK, GMLP_BLOCK), 0) // CHUNK
    c = lax.broadcasted_iota(jnp.int32, (GMLP_BLOCK, GMLP_BLOCK), 1) // CHUNK
    tri = c <= r
    for hh in range(GMLP_HEADS):
        cols = slice(hh * hd, (hh + 1) * hd)
        vn = _layer_norm(v[:, cols], ng_ref[...], nb_ref[...]).astype(BF16)
        w = jnp.where(tri, ws_ref[hh], 0.0).astype(BF16)
        for j in range(t // GMLP_BLOCK):
            rows = slice(j * GMLP_BLOCK, (j + 1) * GMLP_BLOCK)
            sv = jnp.dot(w, vn[rows], preferred_element_type=F32) + bsf_ref[:, cols]
            mix_ref[rows, cols] = (u[rows, cols] * sv * _silu(za[rows, cols])).astype(BF16)

    xb = proj(3)
    zb = proj(4)

    @pl.when(s_idx == 0)
    def _():
        xbuf_ref[0:POOL_HALO, :] = jnp.zeros((POOL_HALO, width), F32)

    xbuf_ref[POOL_HALO:POOL_HALO + t, :] = xb

    def window_sum(cols, win, rows):
        acc = xbuf_ref[POOL_HALO:POOL_HALO + rows, cols]
        for j in range(1, win):
            acc = acc + xbuf_ref[POOL_HALO - j:POOL_HALO - j + rows, cols]
        return acc

    for g, win in enumerate(POOL_WINDOWS):
        cols = slice(g * gd, (g + 1) * gd)
        pooled = window_sum(cols, win, t) * (1.0 / win) - xb[:, cols]
        y = jnp.dot(pooled.astype(BF16), pw_ref[g], preferred_element_type=F32)
        mix_ref[:, width + g * gd: width + (g + 1) * gd] = (
            (y + pb_ref[:, cols]) * ps_ref[:, cols] * _silu(zb[:, cols])).astype(BF16)

    @pl.when(s_idx == 0)
    def _():
        for g, win in enumerate(POOL_WINDOWS):
            cols = slice(g * gd, (g + 1) * gd)
            head = slice(0, POOL_HALO)
            pooled = window_sum(cols, win, POOL_HALO) / cnt_ref[:, cols] - xb[head, cols]
            y = jnp.dot(pooled.astype(BF16), pw_ref[g], preferred_element_type=F32)
            mix_ref[head, width + g * gd: width + (g + 1) * gd] = (
                (y + pb_ref[:, cols]) * ps_ref[:, cols] * _silu(zb[head, cols])).astype(BF16)

    xbuf_ref[0:POOL_HALO, :] = xbuf_ref[t:t + POOL_HALO, :]

    y = jnp.dot(mix_ref[...], w_out_ref[...], preferred_element_type=F32)
    res = DEEPNORM_ALPHA * x + (1.0 + gate_ref[0]) * y
    o_ref[0] = _layer_norm(res, lng_ref[...], lnb_ref[...])


def _layer0(x, shift, scale, gate, w_in, ng, nb, ws, bs, pw, pb, ps, w_out, lng, lnb):
    b, s, d = x.shape
    t = L0_TILE
    width = d
    hd = width // GMLP_HEADS
    gd = width // len(POOL_WINDOWS)
    bsf = jnp.broadcast_to(bs.T[:, :, None], (GMLP_BLOCK, GMLP_HEADS, hd)).reshape(GMLP_BLOCK, width)
    pos = np.arange(POOL_HALO, dtype=np.float32)[:, None] + 1.0
    cnt = np.concatenate(
        [np.broadcast_to(np.minimum(pos, float(w)), (POOL_HALO, gd)) for w in POOL_WINDOWS], axis=1)
    row = lambda a: a.reshape(1, -1)
    vec = pl.BlockSpec((1, 1, d), lambda i, j: (i, 0, 0))
    return pl.pallas_call(
        _l0_kernel,
        grid=(b, s // t),
        in_specs=[
            pl.BlockSpec((1, t, d), lambda i, j: (i, j, 0)),
            vec, vec, vec,
            _resident(w_in.shape),
            _resident((1, hd)), _resident((1, hd)),
            _resident(ws.shape),
            _resident(bsf.shape),
            _resident(pw.shape),
            _resident((1, width)), _resident((1, width)),
            _resident(cnt.shape),
            _resident(w_out.shape),
            _resident((1, d)), _resident((1, d)),
        ],
        out_specs=pl.BlockSpec((1, t, d), lambda i, j: (i, j, 0)),
        out_shape=jax.ShapeDtypeStruct((b, s, d), F32),
        scratch_shapes=[
            pltpu.VMEM((t + POOL_HALO, width), F32),
            pltpu.VMEM((t, 2 * width), BF16),
        ],
        compiler_params=pltpu.CompilerParams(
            dimension_semantics=("arbitrary", "arbitrary"), vmem_limit_bytes=VMEM_LIMIT_BYTES),
        name="layer0_gmlp_pool",
    )(x, shift, scale, gate, w_in.astype(BF16), row(ng), row(nb), ws, bsf,
      pw.astype(BF16), row(pb), row(ps), jnp.asarray(cnt), w_out.astype(BF16), row(lng), row(lnb))


def _fold_kernel(wq_ref, wk_ref, o_ref):
    o_ref[0] = lax.dot_general(wq_ref[0], wk_ref[0], (((1,), (1,)), ((), ())),
                               precision=lax.Precision.HIGHEST, preferred_element_type=F32)


def _fold_query_key(w_uq, w_uk):
    qr = w_uq.shape[0]
    wq = jnp.transpose(w_uq[:, :, :MLA_NOPE], (1, 0, 2))
    wk = jnp.transpose(w_uk, (1, 0, 2))
    return pl.pallas_call(
        _fold_kernel,
        grid=(MLA_HEADS,),
        in_specs=[pl.BlockSpec((1, qr, MLA_NOPE), lambda h: (h, 0, 0)),
                  pl.BlockSpec((1, MLA_KV_RANK, MLA_NOPE), lambda h: (h, 0, 0))],
        out_specs=pl.BlockSpec((1, qr, MLA_KV_RANK), lambda h: (h, 0, 0)),
        out_shape=jax.ShapeDtypeStruct((MLA_HEADS, qr, MLA_KV_RANK), F32),
        name="fold_uq_uk",
    )(wq, wk)


def _rope_kernel(pos_ref, inv_ref, cos_ref, sin_ref):
    ang = pos_ref[...].astype(F32) * inv_ref[...]
    cos_ref[...] = jnp.cos(ang)
    sin_ref[...] = jnp.sin(ang)


def _rope_tables(positions):
    b, s = positions.shape
    half = MLA_ROPE // 2
    per_row = V7X_LANES // half
    rows = b * s // per_row
    inv = (1.0 / (ROPE_THETA ** (np.arange(0, MLA_ROPE, 2, dtype=np.float64) / MLA_ROPE))).astype(np.float32)
    inv_row = jnp.asarray(np.tile(inv, per_row)[None, :])
    pos_dense = jnp.broadcast_to(positions.reshape(rows, per_row, 1), (rows, per_row, half)).reshape(rows, V7X_LANES)
    tr = math.gcd(rows, 1024)
    cos_d, sin_d = pl.pallas_call(
        _rope_kernel,
        grid=(rows // tr,),
        in_specs=[pl.BlockSpec((tr, V7X_LANES), lambda i: (i, 0)), _resident((1, V7X_LANES))],
        out_specs=[pl.BlockSpec((tr, V7X_LANES), lambda i: (i, 0))] * 2,
        out_shape=[jax.ShapeDtypeStruct((rows, V7X_LANES), F32)] * 2,
        name="rope_tables",
    )(pos_dense, inv_row)
    cos = cos_d.reshape(b, s, half)
    sin = sin_d.reshape(b, s, half)
    return jnp.concatenate([cos, sin, cos, sin], axis=-1)


def _l1_proj_kernel(x_ref, shift_ref, scale_ref, cs_ref, w_in_ref, qg_ref, kvg_ref, wq_ref,
                    q_ref, k_ref, zg_ref):
    t = x_ref.shape[1]
    qr, kvr = MLA_Q_RANK, MLA_KV_RANK
    h = (x_ref[0] * (1.0 + scale_ref[0]) + shift_ref[0]).astype(BF16)
    cs = cs_ref[0]

    lat = jnp.dot(h, w_in_ref[:, 0:qr + kvr + V7X_LANES], preferred_element_type=F32)
    q_c = lat[:, 0:qr]
    kv_c = lat[:, qr:qr + kvr]
    k_r = lat[:, qr + kvr:qr + kvr + V7X_LANES]

    q_c = q_c * lax.rsqrt(jnp.mean(q_c * q_c, axis=-1, keepdims=True) + LN_EPS) * qg_ref[...]
    kv_c = kv_c * lax.rsqrt(jnp.mean(kv_c * kv_c, axis=-1, keepdims=True) + LN_EPS) * kvg_ref[...]

    kt = k_r * cs
    quarter = V7X_LANES // 4
    lane = lax.broadcasted_iota(jnp.int32, kt.shape, 1)
    swapped = jnp.where((lane // quarter) % 2 == 0,
                        pltpu.roll(kt, V7X_LANES - quarter, 1), pltpu.roll(kt, quarter, 1))
    k_ref[0] = jnp.concatenate([kv_c, kt + swapped], axis=-1).astype(BF16)

    qn = (q_c * (ATTN_SCALE * LOG2E)).astype(BF16)
    qf = jnp.dot(qn, wq_ref[...], preferred_element_type=F32)
    for hh in range(MLA_HEADS):
        base = hh * HEAD_LANES
        q_lat = qf[:, base:base + V7X_LANES]
        q_rope = qf[:, base + V7X_LANES:base + HEAD_LANES] * cs
        q_ref[0, 0, hh * t:(hh + 1) * t, :] = jnp.concatenate([q_lat, q_rope], axis=-1).astype(BF16)

    z = jnp.dot(h, w_in_ref[:, qr + kvr + V7X_LANES:], preferred_element_type=F32)
    zg_ref[0] = _silu(z).astype(BF16)


def _layer1_proj(x, shift, scale, cs, w_in, qg, kvg, wq):
    b, s, d = x.shape
    t = Q_TILE
    zw = w_in.shape[1] - (MLA_Q_RANK + MLA_KV_RANK + V7X_LANES)
    vec = pl.BlockSpec((1, 1, d), lambda i, j: (i, 0, 0))
    return pl.pallas_call(
        _l1_proj_kernel,
        grid=(b, s // t),
        in_specs=[
            pl.BlockSpec((1, t, d), lambda i, j: (i, j, 0)),
            vec, vec,
            pl.BlockSpec((1, t, V7X_LANES), lambda i, j: (i, j, 0)),
            _resident(w_in.shape),
            _resident((1, MLA_Q_RANK)), _resident((1, MLA_KV_RANK)),
            _resident(wq.shape),
        ],
        out_specs=[
            pl.BlockSpec((1, 1, MLA_HEADS * t, HEAD_LANES), lambda i, j: (i, j, 0, 0)),
            pl.BlockSpec((1, t, HEAD_LANES), lambda i, j: (i, j, 0)),
            pl.BlockSpec((1, t, zw), lambda i, j: (i, j, 0)),
        ],
        out_shape=[
            jax.ShapeDtypeStruct((b, s // t, MLA_HEADS * t, HEAD_LANES), BF16),
            jax.ShapeDtypeStruct((b, s, HEAD_LANES), BF16),
            jax.ShapeDtypeStruct((b, s, zw), BF16),
        ],
        compiler_params=pltpu.CompilerParams(
            dimension_semantics=("arbitrary", "arbitrary"), vmem_limit_bytes=VMEM_LIMIT_BYTES),
        name="layer1_proj",
    )(x, shift, scale, cs, w_in, qg.reshape(1, -1), kvg.reshape(1, -1), wq)


def _attn_kernel(q_ref, k_ref, zg_ref, x_ref, gate_ref, wuv_ref, w_out_ref, lng_ref, lnb_ref,
                 o_ref, m_ref, l_ref, acc_ref, mix_ref):
    qt = x_ref.shape[1]
    rows = q_ref.shape[2]
    tk = K_TILE
    qi = pl.program_id(1)
    reps = tk // V7X_LANES

    m_ref[...] = jnp.full(m_ref.shape, MASK_VALUE, F32)
    l_ref[...] = jnp.zeros(l_ref.shape, F32)
    acc_ref[...] = jnp.zeros(acc_ref.shape, F32)

    def step(kb, key_off):
        start = pl.multiple_of(kb * tk, tk)
        k_blk = k_ref[0, pl.ds(start, tk), :]
        s = lax.dot_general(q_ref[0, 0], k_blk, (((1,), (1,)), ((), ())),
                            preferred_element_type=F32)
        if key_off is not None:
            q_chunk = (lax.broadcasted_iota(jnp.int32, (rows, tk), 0) % qt) // CHUNK
            k_chunk = (lax.broadcasted_iota(jnp.int32, (rows, tk), 1) + key_off) // CHUNK
            s = jnp.where(k_chunk <= q_chunk, s, MASK_VALUE)
        m_prev = m_ref[...]
        m_new = jnp.maximum(m_prev, jnp.max(s, axis=1, keepdims=True))
        alpha = jnp.exp2(m_prev - m_new)
        p = jnp.exp2(s - jnp.concatenate([m_new] * reps, axis=1))
        l_ref[...] = alpha * l_ref[...] + jnp.sum(p, axis=1, keepdims=True)
        m_ref[...] = m_new
        pv = jnp.dot(p.astype(BF16), k_blk[:, 0:MLA_KV_RANK], preferred_element_type=F32)
        acc_ref[...] = alpha * acc_ref[...] + pv

    n_full = (qi * qt) // tk

    def body(kb, carry):
        step(kb, None)
        return carry

    lax.fori_loop(0, n_full, body, 0)
    for d in range(qt // tk):
        step(n_full + d, d * tk)

    o_lat = (acc_ref[...] / l_ref[...]).astype(BF16)
    for hh in range(MLA_HEADS):
        o_h = jnp.dot(o_lat[hh * qt:(hh + 1) * qt], wuv_ref[hh], preferred_element_type=F32)
        cols = slice(hh * MLA_V, (hh + 1) * MLA_V)
        mix_ref[:, cols] = (o_h * zg_ref[0, :, cols].astype(F32)).astype(BF16)
    y = jnp.dot(mix_ref[...], w_out_ref[...], preferred_element_type=F32)
    res = DEEPNORM_ALPHA * x_ref[0] + (1.0 + gate_ref[0]) * y
    o_ref[0] = _layer_norm(res, lng_ref[...], lnb_ref[...])


def _layer1_attn(q, k, zg, x, gate, wuv, w_out, lng, lnb):
    b, s, d = x.shape
    qt = Q_TILE
    rows = MLA_HEADS * qt
    width = zg.shape[2]
    return pl.pallas_call(
        _attn_kernel,
        grid=(b, s // qt),
        in_specs=[
            pl.BlockSpec((1, 1, rows, HEAD_LANES), lambda i, j: (i, j, 0, 0)),
            pl.BlockSpec((1, s, HEAD_LANES), lambda i, j: (i, 0, 0)),
            pl.BlockSpec((1, qt, width), lambda i, j: (i, j, 0)),
            pl.BlockSpec((1, qt, d), lambda i, j: (i, j, 0)),
            pl.BlockSpec((1, 1, d), lambda i, j: (i, 0, 0)),
            _resident(wuv.shape),
            _resident(w_out.shape),
            _resident((1, d)), _resident((1, d)),
        ],
        out_specs=pl.BlockSpec((1, qt, d), lambda i, j: (i, j, 0)),
        out_shape=jax.ShapeDtypeStruct((b, s, d), F32),
        scratch_shapes=[
            pltpu.VMEM((rows, V7X_LANES), F32),
            pltpu.VMEM((rows, V7X_LANES), F32),
            pltpu.VMEM((rows, MLA_KV_RANK), F32),
            pltpu.VMEM((qt, width), BF16),
        ],
        compiler_params=pltpu.CompilerParams(
            dimension_semantics=("arbitrary", "arbitrary"), vmem_limit_bytes=VMEM_LIMIT_BYTES),
        name="layer1_attn",
    )(q, k, zg, x, gate, wuv, w_out, lng.reshape(1, -1), lnb.reshape(1, -1))


def _layer1_weights(w_in, w_uq, w_uk, w_uv):
    qr, kvr, half = MLA_Q_RANK, MLA_KV_RANK, MLA_ROPE // 2
    k1 = w_in[:, qr + kvr:qr + kvr + half]
    k2 = w_in[:, qr + kvr + half:qr + kvr + MLA_ROPE]
    w_in_p = jnp.concatenate(
        [w_in[:, :qr + kvr], k1, -k2, k2, k1, w_in[:, qr + kvr + MLA_ROPE:]], axis=1).astype(BF16)
    w_lat = jnp.transpose(_fold_query_key(w_uq, w_uk), (1, 0, 2))
    r1 = w_uq[:, :, MLA_NOPE:MLA_NOPE + half]
    r2 = w_uq[:, :, MLA_NOPE + half:]
    wq = jnp.concatenate([w_lat, r1, -r2, r2, r1], axis=2).reshape(qr, MLA_HEADS * HEAD_LANES).astype(BF16)
    wuv = jnp.transpose(w_uv, (1, 0, 2)).astype(BF16)
    return w_in_p, wq, wuv


def kernel(x, c, positions, ada_w, ada_b, ln_g, ln_b, e_w_in, gmlp_norm_g, gmlp_norm_b, gmlp_ws,
           gmlp_bs, pool_w, pool_b, pool_scale, e_w_out, o_w_in, mla_q_norm_g, mla_kv_norm_g,
           mla_w_uq, mla_w_uk, mla_w_uv, o_w_out):
    b, s, d = x.shape
    assert s % L0_TILE == 0 and s % Q_TILE == 0 and Q_TILE % K_TILE == 0 and K_TILE % CHUNK == 0
    assert L0_TILE % GMLP_BLOCK == 0 and ada_w.shape[0] == DEPTH

    mod = _modulation(c, ada_w, ada_b)
    shift, scale, gate = (mod[:, :, i * d:(i + 1) * d].reshape(DEPTH, b, 1, d) for i in range(3))

    x = _layer0(x, shift[0], scale[0], gate[0], e_w_in[0], gmlp_norm_g[0], gmlp_norm_b[0], gmlp_ws[0],
                gmlp_bs[0], pool_w[0], pool_b[0], pool_scale[0], e_w_out[0], ln_g[0], ln_b[0])

    w_in_p, wq, wuv = _layer1_weights(o_w_in[0], mla_w_uq[0], mla_w_uk[0], mla_w_uv[0])
    cs = _rope_tables(positions)
    q, k, zg = _layer1_proj(x, shift[1], scale[1], cs, w_in_p, mla_q_norm_g[0], mla_kv_norm_g[0], wq)
    return _layer1_attn(q, k, zg, x, gate[1], wuv, o_w_out[0].astype(BF16), ln_g[1], ln_b[1])
```

```python
import functools
import math

import numpy as np
import jax
import jax.numpy as jnp
from jax import lax
from jax.experimental import pallas as pl
from jax.experimental.pallas import tpu as pltpu

F32 = jnp.float32
BF16 = jnp.bfloat16

CHUNK = 64
LN_EPS = 1e-5
GMLP_HEADS = 4
GMLP_BLOCK = 128
POOL_WINDOWS = (2, 4, 8, 16)
MLA_HEADS = 16
MLA_NOPE = 128
MLA_ROPE = 64
MLA_V = 128
MLA_Q_RANK = 256
MLA_KV_RANK = 128
ROPE_THETA = 10000.0
ATTN_SCALE = (MLA_NOPE + MLA_ROPE) ** -0.5
DEPTH = 2
DEEPNORM_ALPHA = (2.0 * DEPTH) ** 0.25

V7X_LANES = 128
V7X_VMEM_BYTES = 64 * 1024 * 1024
VMEM_LIMIT_BYTES = V7X_VMEM_BYTES - 8 * 1024 * 1024

POOL_HALO = max(POOL_WINDOWS)
HEAD_LANES = 2 * V7X_LANES
MASK_VALUE = -1e30
LOG2E = math.log2(math.e)

L0_TILE = 256
Q_TILE = 256
BF16_SUBLANES = 16
VT_ROWS = MLA_KV_RANK + BF16_SUBLANES


def _silu(v):
    return v * jax.nn.sigmoid(v)


def _layer_norm(v, g, b):
    mu = jnp.mean(v, axis=-1, keepdims=True)
    d = v - mu
    var = jnp.mean(d * d, axis=-1, keepdims=True)
    return d * lax.rsqrt(var + LN_EPS) * g + b


def _resident(shape):
    nd = len(shape)
    return pl.BlockSpec(shape, lambda *_: (0,) * nd)


def _mod_kernel(c_ref, w_ref, b_ref, o_ref):
    cond = _silu(c_ref[...]).astype(BF16)
    o_ref[0] = jnp.dot(cond, w_ref[0].astype(BF16), preferred_element_type=F32) + b_ref[0]


def _modulation(c, ada_w, ada_b):
    depth, d, n = ada_w.shape
    b = c.shape[0]
    tn = d
    return pl.pallas_call(
        _mod_kernel,
        grid=(depth, n // tn),
        in_specs=[
            pl.BlockSpec((b, d), lambda l, j: (0, 0)),
            pl.BlockSpec((1, d, tn), lambda l, j: (l, 0, j)),
            pl.BlockSpec((1, 1, tn), lambda l, j: (l, 0, j)),
        ],
        out_specs=pl.BlockSpec((1, b, tn), lambda l, j: (l, 0, j)),
        out_shape=jax.ShapeDtypeStruct((depth, b, n), F32),
        compiler_params=pltpu.CompilerParams(
            dimension_semantics=("arbitrary", "arbitrary"), vmem_limit_bytes=VMEM_LIMIT_BYTES),
        name="adaln_mod",
    )(c, ada_w, ada_b.reshape(depth, 1, n))


def _l0_kernel(x_ref, shift_ref, scale_ref, gate_ref, w_in_ref, ng_ref, nb_ref, ws_ref, bsf_ref,
               pw_ref, pb_ref, ps_ref, cnt_ref, w_out_ref, lng_ref, lnb_ref,
               o_ref, xbuf_ref, mix_ref):
    t, d = x_ref.shape[1], x_ref.shape[2]
    width = d
    hd = width // GMLP_HEADS
    gd = width // len(POOL_WINDOWS)
    s_idx = pl.program_id(1)

    x = x_ref[0]
    h = (x * (1.0 + scale_ref[0]) + shift_ref[0]).astype(BF16)

    def proj(i):
        return jnp.dot(h, w_in_ref[:, i * width:(i + 1) * width], preferred_element_type=F32)

    u = proj(0)
    v = proj(1)
    za = proj(2)
    r = lax.broadcasted_iota(jnp.int32, (GMLP_BLOCK, GMLP_BLOCK), 0) // CHUNK
    c = lax.broadcasted_iota(jnp.int32, (GMLP_BLOCK, GMLP_BLOCK), 1) // CHUNK
    tri = c <= r
    for hh in range(GMLP_HEADS):
        cols = slice(hh * hd, (hh + 1) * hd)
        vn = _layer_norm(v[:, cols], ng_ref[...], nb_ref[...]).astype(BF16)
        w = jnp.where(tri, ws_ref[hh], 0.0).astype(BF16)
        for j in range(t // GMLP_BLOCK):
            rows = slice(j * GMLP_BLOCK, (j + 1) * GMLP_BLOCK)
            sv = jnp.dot(w, vn[rows], preferred_element_type=F32) + bsf_ref[:, cols]
            mix_ref[rows, cols] = (u[rows, cols] * sv * _silu(za[rows, cols])).astype(BF16)

    xb = proj(3)
    zb = proj(4)

    @pl.when(s_idx == 0)
    def _():
        xbuf_ref[0:POOL_HALO, :] = jnp.zeros((POOL_HALO, width), F32)

    xbuf_ref[POOL_HALO:POOL_HALO + t, :] = xb

    def window_sum(cols, win, rows):
        acc = xbuf_ref[POOL_HALO:POOL_HALO + rows, cols]
        for j in range(1, win):
            acc = acc + xbuf_ref[POOL_HALO - j:POOL_HALO - j + rows, cols]
        return acc

    for g, win in enumerate(POOL_WINDOWS):
        cols = slice(g * gd, (g + 1) * gd)
        pooled = window_sum(cols, win, t) * (1.0 / win) - xb[:, cols]
        y = jnp.dot(pooled.astype(BF16), pw_ref[g], preferred_element_type=F32)
        mix_ref[:, width + g * gd: width + (g + 1) * gd] = (
            (y + pb_ref[:, cols]) * ps_ref[:, cols] * _silu(zb[:, cols])).astype(BF16)

    @pl.when(s_idx == 0)
    def _():
        for g, win in enumerate(POOL_WINDOWS):
            cols = slice(g * gd, (g + 1) * gd)
            head = slice(0, POOL_HALO)
            pooled = window_sum(cols, win, POOL_HALO) / cnt_ref[:, cols] - xb[head, cols]
            y = jnp.dot(pooled.astype(BF16), pw_ref[g], preferred_element_type=F32)
            mix_ref[head, width + g * gd: width + (g + 1) * gd] = (
                (y + pb_ref[:, cols]) * ps_ref[:, cols] * _silu(zb[head, cols])).astype(BF16)

    xbuf_ref[0:POOL_HALO, :] = xbuf_ref[t:t + POOL_HALO, :]

    y = jnp.dot(mix_ref[...], w_out_ref[...], preferred_element_type=F32)
    res = DEEPNORM_ALPHA * x + (1.0 + gate_ref[0]) * y
    o_ref[0] = _layer_norm(res, lng_ref[...], lnb_ref[...])


def _layer0(x, shift, scale, gate, w_in, ng, nb, ws, bs, pw, pb, ps, w_out, lng, lnb):
    b, s, d = x.shape
    t = L0_TILE
    width = d
    hd = width // GMLP_HEADS
    gd = width // len(POOL_WINDOWS)
    bsf = jnp.broadcast_to(bs.T[:, :, None], (GMLP_BLOCK, GMLP_HEADS, hd)).reshape(GMLP_BLOCK, width)
    pos = np.arange(POOL_HALO, dtype=np.float32)[:, None] + 1.0
    cnt = np.concatenate(
        [np.broadcast_to(np.minimum(pos, float(w)), (POOL_HALO, gd)) for w in POOL_WINDOWS], axis=1)
    row = lambda a: a.reshape(1, -1)
    vec = pl.BlockSpec((1, 1, d), lambda i, j: (i, 0, 0))
    return pl.pallas_call(
        _l0_kernel,
        grid=(b, s // t),
        in_specs=[
            pl.BlockSpec((1, t, d), lambda i, j: (i, j, 0)),
            vec, vec, vec,
            _resident(w_in.shape),
            _resident((1, hd)), _resident((1, hd)),
            _resident(ws.shape),
            _resident(bsf.shape),
            _resident(pw.shape),
            _resident((1, width)), _resident((1, width)),
            _resident(cnt.shape),
            _resident(w_out.shape),
            _resident((1, d)), _resident((1, d)),
        ],
        out_specs=pl.BlockSpec((1, t, d), lambda i, j: (i, j, 0)),
        out_shape=jax.ShapeDtypeStruct((b, s, d), F32),
        scratch_shapes=[
            pltpu.VMEM((t + POOL_HALO, width), F32),
            pltpu.VMEM((t, 2 * width), BF16),
        ],
        compiler_params=pltpu.CompilerParams(
            dimension_semantics=("arbitrary", "arbitrary"), vmem_limit_bytes=VMEM_LIMIT_BYTES),
        name="layer0_gmlp_pool",
    )(x, shift, scale, gate, w_in.astype(BF16), row(ng), row(nb), ws, bsf,
      pw.astype(BF16), row(pb), row(ps), jnp.asarray(cnt), w_out.astype(BF16), row(lng), row(lnb))


def _fold_kernel(wq_ref, wk_ref, o_ref):
    o_ref[0] = lax.dot_general(wq_ref[0], wk_ref[0], (((1,), (1,)), ((), ())),
                               precision=lax.Precision.HIGHEST, preferred_element_type=F32)


def _fold_query_key(w_uq, w_uk):
    qr = w_uq.shape[0]
    wq = jnp.transpose(w_uq[:, :, :MLA_NOPE], (1, 0, 2))
    wk = jnp.transpose(w_uk, (1, 0, 2))
    return pl.pallas_call(
        _fold_kernel,
        grid=(MLA_HEADS,),
        in_specs=[pl.BlockSpec((1, qr, MLA_NOPE), lambda h: (h, 0, 0)),
                  pl.BlockSpec((1, MLA_KV_RANK, MLA_NOPE), lambda h: (h, 0, 0))],
        out_specs=pl.BlockSpec((1, qr, MLA_KV_RANK), lambda h: (h, 0, 0)),
        out_shape=jax.ShapeDtypeStruct((MLA_HEADS, qr, MLA_KV_RANK), F32),
        name="fold_uq_uk",
    )(wq, wk)


def _rope_kernel(pos_ref, inv_ref, cos_ref, sin_ref):
    ang = pos_ref[...].astype(F32) * inv_ref[...]
    cos_ref[...] = jnp.cos(ang)
    sin_ref[...] = jnp.sin(ang)


def _rope_tables(positions):
    b, s = positions.shape
    half = MLA_ROPE // 2
    per_row = V7X_LANES // half
    rows = b * s // per_row
    inv = (1.0 / (ROPE_THETA ** (np.arange(0, MLA_ROPE, 2, dtype=np.float64) / MLA_ROPE))).astype(np.float32)
    inv_row = jnp.asarray(np.tile(inv, per_row)[None, :])
    pos_dense = jnp.broadcast_to(positions.reshape(rows, per_row, 1), (rows, per_row, half)).reshape(rows, V7X_LANES)
    tr = math.gcd(rows, 1024)
    cos_d, sin_d = pl.pallas_call(
        _rope_kernel,
        grid=(rows // tr,),
        in_specs=[pl.BlockSpec((tr, V7X_LANES), lambda i: (i, 0)), _resident((1, V7X_LANES))],
        out_specs=[pl.BlockSpec((tr, V7X_LANES), lambda i: (i, 0))] * 2,
        out_shape=[jax.ShapeDtypeStruct((rows, V7X_LANES), F32)] * 2,
        name="rope_tables",
    )(pos_dense, inv_row)
    cos = cos_d.reshape(b, s, half)
    sin = sin_d.reshape(b, s, half)
    return jnp.concatenate([cos, sin, cos, sin], axis=-1)


def _l1_proj_kernel(x_ref, shift_ref, scale_ref, cs_ref, w_in_ref, qg_ref, kvg_ref, wq_ref,
                    q_ref, k_ref, vt_ref, zg_ref):
    t = x_ref.shape[1]
    qr, kvr = MLA_Q_RANK, MLA_KV_RANK
    h = (x_ref[0] * (1.0 + scale_ref[0]) + shift_ref[0]).astype(BF16)
    cs = cs_ref[0]

    lat = jnp.dot(h, w_in_ref[:, 0:qr + kvr + V7X_LANES], preferred_element_type=F32)
    q_c = lat[:, 0:qr]
    kv_c = lat[:, qr:qr + kvr]
    k_r = lat[:, qr + kvr:qr + kvr + V7X_LANES]

    q_c = q_c * lax.rsqrt(jnp.mean(q_c * q_c, axis=-1, keepdims=True) + LN_EPS) * qg_ref[...]
    kv_c = kv_c * lax.rsqrt(jnp.mean(kv_c * kv_c, axis=-1, keepdims=True) + LN_EPS) * kvg_ref[...]

    kt = k_r * cs
    quarter = V7X_LANES // 4
    lane = lax.broadcasted_iota(jnp.int32, kt.shape, 1)
    swapped = jnp.where((lane // quarter) % 2 == 0,
                        pltpu.roll(kt, V7X_LANES - quarter, 1), pltpu.roll(kt, quarter, 1))
    k_ref[0] = jnp.concatenate([kv_c, kt + swapped], axis=-1).astype(BF16)
    vt_ref[0, 0:kvr, :] = kv_c.T.astype(BF16)
    vt_ref[0, kvr:VT_ROWS, :] = jnp.ones((VT_ROWS - kvr, t), BF16)

    qn_t = (q_c * (ATTN_SCALE * LOG2E)).T.astype(BF16)
    qf_t = jnp.dot(wq_ref[...], qn_t, preferred_element_type=F32)
    cs_t = cs.T
    for hh in range(MLA_HEADS):
        base = hh * HEAD_LANES
        q_lat = qf_t[base:base + V7X_LANES]
        q_rope = qf_t[base + V7X_LANES:base + HEAD_LANES] * cs_t
        q_ref[0, 0, :, hh * t:(hh + 1) * t] = jnp.concatenate([q_lat, q_rope], axis=0).astype(BF16)

    z = jnp.dot(h, w_in_ref[:, qr + kvr + V7X_LANES:], preferred_element_type=F32)
    zg_ref[0] = _silu(z).astype(BF16)


def _layer1_proj(x, shift, scale, cs, w_in, qg, kvg, wq):
    b, s, d = x.shape
    t = Q_TILE
    zw = w_in.shape[1] - (MLA_Q_RANK + MLA_KV_RANK + V7X_LANES)
    vec = pl.BlockSpec((1, 1, d), lambda i, j: (i, 0, 0))
    return pl.pallas_call(
        _l1_proj_kernel,
        grid=(b, s // t),
        in_specs=[
            pl.BlockSpec((1, t, d), lambda i, j: (i, j, 0)),
            vec, vec,
            pl.BlockSpec((1, t, V7X_LANES), lambda i, j: (i, j, 0)),
            _resident(w_in.shape),
            _resident((1, MLA_Q_RANK)), _resident((1, MLA_KV_RANK)),
            _resident(wq.shape),
        ],
        out_specs=[
            pl.BlockSpec((1, 1, HEAD_LANES, MLA_HEADS * t), lambda i, j: (i, j, 0, 0)),
            pl.BlockSpec((1, t, HEAD_LANES), lambda i, j: (i, j, 0)),
            pl.BlockSpec((1, VT_ROWS, t), lambda i, j: (i, 0, j)),
            pl.BlockSpec((1, t, zw), lambda i, j: (i, j, 0)),
        ],
        out_shape=[
            jax.ShapeDtypeStruct((b, s // t, HEAD_LANES, MLA_HEADS * t), BF16),
            jax.ShapeDtypeStruct((b, s, HEAD_LANES), BF16),
            jax.ShapeDtypeStruct((b, VT_ROWS, s), BF16),
            jax.ShapeDtypeStruct((b, s, zw), BF16),
        ],
        compiler_params=pltpu.CompilerParams(
            dimension_semantics=("arbitrary", "arbitrary"), vmem_limit_bytes=VMEM_LIMIT_BYTES),
        name="layer1_proj",
    )(x, shift, scale, cs, w_in, qg.reshape(1, -1), kvg.reshape(1, -1), wq)


def _attn_kernel(q_ref, k_ref, vt_ref, zg_ref, x_ref, gate_ref, wuv_ref, w_out_ref, lng_ref, lnb_ref,
                 o_ref, s0_ref, s1_ref, mc0_ref, mc1_ref, m_ref, acc_ref, mix_ref):
    qt = x_ref.shape[1]
    rows = q_ref.shape[3]
    qi = pl.program_id(1)

    def scores(start, s_ref, mc_ref, masked):
        start = pl.multiple_of(start, qt)
        st = jnp.dot(k_ref[0, pl.ds(start, qt), :], q_ref[0, 0], preferred_element_type=F32)
        if masked:
            k_chunk = lax.broadcasted_iota(jnp.int32, (qt, rows), 0) // CHUNK
            q_chunk = (lax.broadcasted_iota(jnp.int32, (qt, rows), 1) % qt) // CHUNK
            st = jnp.where(k_chunk <= q_chunk, st, MASK_VALUE)
        s_ref[...] = st
        mc_ref[...] = jnp.max(st, axis=0, keepdims=True)

    def accumulate(start, s_ref, mc_ref):
        start = pl.multiple_of(start, qt)
        m_prev = m_ref[...]
        m_new = jnp.maximum(m_prev, mc_ref[...])
        alpha = jnp.exp2(m_prev - m_new)
        pt = jnp.exp2(s_ref[...] - m_new).astype(BF16)
        pv = jnp.dot(vt_ref[0, :, pl.ds(start, qt)], pt, preferred_element_type=F32)
        acc_ref[...] = alpha * acc_ref[...] + pv
        m_ref[...] = m_new

    m_ref[...] = jnp.full(m_ref.shape, MASK_VALUE, F32)
    acc_ref[...] = jnp.zeros(acc_ref.shape, F32)

    slot0, slot1 = (s0_ref, mc0_ref), (s1_ref, mc1_ref)

    def trip(i, into, fold):
        scores(i * qt, *into, False)
        accumulate(jnp.where(i == 0, qi, i - 1) * qt, *fold)

    def pair(p, carry):
        trip(2 * p, slot1, slot0)
        trip(2 * p + 1, slot0, slot1)
        return carry

    scores(qi * qt, *slot0, True)
    lax.fori_loop(0, qi // 2, pair, 0)

    @pl.when(qi % 2 == 1)
    def _():
        trip(qi - 1, slot1, slot0)
        accumulate((qi - 1) * qt, *slot1)

    @pl.when(qi % 2 == 0)
    def _():
        accumulate(jnp.maximum(qi - 1, 0) * qt, *slot0)

    inv_l = 1.0 / acc_ref[MLA_KV_RANK:MLA_KV_RANK + 1, :]
    o_t = (acc_ref[0:MLA_KV_RANK, :] * inv_l).astype(BF16)
    for hh in range(MLA_HEADS):
        o_h = lax.dot_general(o_t[:, hh * qt:(hh + 1) * qt], wuv_ref[hh], (((0,), (0,)), ((), ())),
                              preferred_element_type=F32)
        cols = slice(hh * MLA_V, (hh + 1) * MLA_V)
        mix_ref[:, cols] = (o_h * zg_ref[0, :, cols].astype(F32)).astype(BF16)
    y = jnp.dot(mix_ref[...], w_out_ref[...], preferred_element_type=F32)
    res = DEEPNORM_ALPHA * x_ref[0] + (1.0 + gate_ref[0]) * y
    o_ref[0] = _layer_norm(res, lng_ref[...], lnb_ref[...])


def _layer1_attn(q, k, vt, zg, x, gate, wuv, w_out, lng, lnb):
    b, s, d = x.shape
    qt = Q_TILE
    rows = MLA_HEADS * qt
    width = zg.shape[2]
    return pl.pallas_call(
        _attn_kernel,
        grid=(b, s // qt),
        in_specs=[
            pl.BlockSpec((1, 1, HEAD_LANES, rows), lambda i, j: (i, j, 0, 0)),
            pl.BlockSpec((1, s, HEAD_LANES), lambda i, j: (i, 0, 0)),
            pl.BlockSpec((1, VT_ROWS, s), lambda i, j: (i, 0, 0)),
            pl.BlockSpec((1, qt, width), lambda i, j: (i, j, 0)),
            pl.BlockSpec((1, qt, d), lambda i, j: (i, j, 0)),
            pl.BlockSpec((1, 1, d), lambda i, j: (i, 0, 0)),
            _resident(wuv.shape),
            _resident(w_out.shape),
            _resident((1, d)), _resident((1, d)),
        ],
        out_specs=pl.BlockSpec((1, qt, d), lambda i, j: (i, j, 0)),
        out_shape=jax.ShapeDtypeStruct((b, s, d), F32),
        scratch_shapes=[
            pltpu.VMEM((qt, rows), F32),
            pltpu.VMEM((qt, rows), F32),
            pltpu.VMEM((1, rows), F32),
            pltpu.VMEM((1, rows), F32),
            pltpu.VMEM((1, rows), F32),
            pltpu.VMEM((VT_ROWS, rows), F32),
            pltpu.VMEM((qt, width), BF16),
        ],
        compiler_params=pltpu.CompilerParams(
            dimension_semantics=("arbitrary", "arbitrary"), vmem_limit_bytes=VMEM_LIMIT_BYTES),
        name="layer1_attn",
    )(q, k, vt, zg, x, gate, wuv, w_out, lng.reshape(1, -1), lnb.reshape(1, -1))


def _layer1_weights(w_in, w_uq, w_uk, w_uv):
    qr, kvr, half = MLA_Q_RANK, MLA_KV_RANK, MLA_ROPE // 2
    k1 = w_in[:, qr + kvr:qr + kvr + half]
    k2 = w_in[:, qr + kvr + half:qr + kvr + MLA_ROPE]
    w_in_p = jnp.concatenate(
        [w_in[:, :qr + kvr], k1, -k2, k2, k1, w_in[:, qr + kvr + MLA_ROPE:]], axis=1).astype(BF16)
    w_lat = jnp.transpose(_fold_query_key(w_uq, w_uk), (1, 0, 2))
    r1 = w_uq[:, :, MLA_NOPE:MLA_NOPE + half]
    r2 = w_uq[:, :, MLA_NOPE + half:]
    wq = jnp.concatenate([w_lat, r1, -r2, r2, r1], axis=2).reshape(qr, MLA_HEADS * HEAD_LANES).T.astype(BF16)
    wuv = jnp.transpose(w_uv, (1, 0, 2)).astype(BF16)
    return w_in_p, wq, wuv


def kernel(x, c, positions, ada_w, ada_b, ln_g, ln_b, e_w_in, gmlp_norm_g, gmlp_norm_b, gmlp_ws,
           gmlp_bs, pool_w, pool_b, pool_scale, e_w_out, o_w_in, mla_q_norm_g, mla_kv_norm_g,
           mla_w_uq, mla_w_uk, mla_w_uv, o_w_out):
    b, s, d = x.shape
    assert s % L0_TILE == 0 and s % Q_TILE == 0 and Q_TILE % CHUNK == 0
    assert L0_TILE % GMLP_BLOCK == 0 and ada_w.shape[0] == DEPTH

    mod = _modulation(c, ada_w, ada_b)
    shift, scale, gate = (mod[:, :, i * d:(i + 1) * d].reshape(DEPTH, b, 1, d) for i in range(3))

    x = _layer0(x, shift[0], scale[0], gate[0], e_w_in[0], gmlp_norm_g[0], gmlp_norm_b[0], gmlp_ws[0],
                gmlp_bs[0], pool_w[0], pool_b[0], pool_scale[0], e_w_out[0], ln_g[0], ln_b[0])

    w_in_p, wq, wuv = _layer1_weights(o_w_in[0], mla_w_uq[0], mla_w_uk[0], mla_w_uv[0])
    cs = _rope_tables(positions)
    q, k, vt, zg = _layer1_proj(x, shift[1], scale[1], cs, w_in_p, mla_q_norm_g[0], mla_kv_norm_g[0], wq)
    return _layer1_attn(q, k, vt, zg, x, gate[1], wuv, o_w_out[0].astype(BF16), ln_g[1], ln_b[1])
```

```python
import functools
import math

import numpy as np
import jax
import jax.numpy as jnp
from jax import lax
from jax.experimental import pallas as pl
from jax.experimental.pallas import tpu as pltpu

F32 = jnp.float32
BF16 = jnp.bfloat16

CHUNK = 64
LN_EPS = 1e-5
GMLP_HEADS = 4
GMLP_BLOCK = 128
POOL_WINDOWS = (2, 4, 8, 16)
MLA_HEADS = 16
MLA_NOPE = 128
MLA_ROPE = 64
MLA_V = 128
MLA_Q_RANK = 256
MLA_KV_RANK = 128
ROPE_THETA = 10000.0
ATTN_SCALE = (MLA_NOPE + MLA_ROPE) ** -0.5
DEPTH = 2
DEEPNORM_ALPHA = (2.0 * DEPTH) ** 0.25

V7X_LANES = 128
V7X_MXU_WIDTH = 256
V7X_VMEM_BYTES = 64 * 1024 * 1024
VMEM_LIMIT_BYTES = V7X_VMEM_BYTES - 8 * 1024 * 1024

POOL_HALO = max(POOL_WINDOWS)
V7X_SUBLANES = 8
POOL_PAD = V7X_SUBLANES
HEAD_LANES = 2 * V7X_LANES
MASK_VALUE = -1e30
LOG2E = math.log2(math.e)

L0_TILE = 256
Q_TILE = 256
ATTN_COL_GROUPS = MLA_HEADS
EPILOGUE_LAG = 4
PROJ_PIECES = 4
BF16_SUBLANES = 16
VT_ROWS = MLA_KV_RANK + BF16_SUBLANES


def _silu(v):
    return v * jax.nn.sigmoid(v)


def _layer_norm(v, g, b):
    mu = jnp.mean(v, axis=-1, keepdims=True)
    d = v - mu
    var = jnp.mean(d * d, axis=-1, keepdims=True)
    return d * lax.rsqrt(var + LN_EPS) * g + b


def _resident(shape):
    nd = len(shape)
    return pl.BlockSpec(shape, lambda *_: (0,) * nd, pipeline_mode=pl.Buffered(1))


def _mod_kernel(c_ref, w_ref, b_ref, o_ref):
    cond = _silu(c_ref[...]).astype(BF16)
    o_ref[0] = jnp.dot(cond, w_ref[0].astype(BF16), preferred_element_type=F32) + b_ref[0]


def _modulation(c, ada_w, ada_b):
    depth, d, n = ada_w.shape
    b = c.shape[0]
    tn = d
    return pl.pallas_call(
        _mod_kernel,
        grid=(depth, n // tn),
        in_specs=[
            pl.BlockSpec((b, d), lambda l, j: (0, 0)),
            pl.BlockSpec((1, d, tn), lambda l, j: (l, 0, j)),
            pl.BlockSpec((1, 1, tn), lambda l, j: (l, 0, j)),
        ],
        out_specs=pl.BlockSpec((1, b, tn), lambda l, j: (l, 0, j)),
        out_shape=jax.ShapeDtypeStruct((depth, b, n), F32),
        compiler_params=pltpu.CompilerParams(
            dimension_semantics=("arbitrary", "arbitrary"), vmem_limit_bytes=VMEM_LIMIT_BYTES),
        name="adaln_mod",
    )(c, ada_w, ada_b.reshape(depth, 1, n))


def _l0_kernel(xp_ref, xn_ref, shc_ref, scc_ref, shn_ref, scn_ref, gate_ref, w_in_ref, ng_ref, nb_ref,
               ws_ref, bsf_ref, pw_ref, pb_ref, ps_ref, cnt_ref, w_out_ref, lng_ref, lnb_ref,
               o_ref, p0_ref, p1_ref, xbuf_ref, la_ref, lb_ref, mix_ref):
    t, d = xn_ref.shape[1], xn_ref.shape[2]
    width = d
    hd = width // GMLP_HEADS
    gd = width // len(POOL_WINDOWS)
    halo = POOL_PAD + POOL_HALO
    first_step = jnp.logical_and(pl.program_id(0) == 0, pl.program_id(1) == 0)
    seq_start = pl.program_id(1) == 0
    n_pieces = GMLP_HEADS + len(POOL_WINDOWS)
    mxu_cols = w_in_ref.shape[1] // V7X_MXU_WIDTH
    cuts = [V7X_MXU_WIDTH * ((k * mxu_cols) // n_pieces) for k in range(n_pieces + 1)]

    def modulate(x, shift, scale):
        return (x * (1.0 + scale) + shift).astype(BF16)

    def project_piece(h, p_ref, k):
        cols = slice(cuts[k], cuts[k + 1])
        p_ref[:, cols] = jnp.dot(h, w_in_ref[:, cols], preferred_element_type=F32)

    r = lax.broadcasted_iota(jnp.int32, (GMLP_BLOCK, GMLP_BLOCK), 0) // CHUNK
    c = lax.broadcasted_iota(jnp.int32, (GMLP_BLOCK, GMLP_BLOCK), 1) // CHUNK
    tri = c <= r

    def gmlp_head(p_ref, hh):
        cols = slice(hh * hd, (hh + 1) * hd)
        vn = _layer_norm(p_ref[:, width + hh * hd:width + (hh + 1) * hd],
                         ng_ref[...], nb_ref[...]).astype(BF16)
        w = jnp.where(tri, ws_ref[hh], 0.0).astype(BF16)
        for k in range(t // GMLP_BLOCK):
            rows = slice(k * GMLP_BLOCK, (k + 1) * GMLP_BLOCK)
            sv = jnp.dot(w, vn[rows], preferred_element_type=F32) + bsf_ref[:, cols]
            za = p_ref[rows, 2 * width + hh * hd:2 * width + (hh + 1) * hd]
            mix_ref[rows, cols] = (p_ref[rows, cols] * sv * _silu(za)).astype(BF16)

    def pool_fill(p_ref, is_seq_start):
        carry = xbuf_ref[POOL_PAD:halo, :]
        if is_seq_start is not None:
            carry = jnp.where(is_seq_start, 0.0, carry)
            xbuf_ref[POOL_PAD:halo, :] = carry
        xbuf_ref[halo:halo + t, :] = p_ref[:, 3 * width:4 * width]

    def pool_tail(p_ref, g, pooled, rows):
        cols = slice(g * gd, (g + 1) * gd)
        y = jnp.dot(pooled.astype(BF16), pw_ref[g], preferred_element_type=F32)
        zb = p_ref[rows, 4 * width + g * gd:4 * width + (g + 1) * gd]
        mix_ref[rows, width + g * gd:width + (g + 1) * gd] = (
            (y + pb_ref[:, cols]) * ps_ref[:, cols] * _silu(zb)).astype(BF16)

    def pool_group(p_ref, g, is_seq_start):
        win = POOL_WINDOWS[g]
        cols = slice(g * gd, (g + 1) * gd)
        levels = win.bit_length() - 1
        src = xbuf_ref.at[:, cols]
        for lvl in range(1, levels + 1):
            shift = 1 << (lvl - 1)
            lo = halo if lvl == levels else POOL_PAD
            n = halo + t - lo
            val = src[lo:lo + n, :] + src[lo - shift:lo - shift + n, :]
            if lvl < levels:
                dst = la_ref if lvl % 2 else lb_ref
                dst[lo:lo + n, :] = val
                src = dst
        tok = xbuf_ref[halo:halo + t, cols]
        pool_tail(p_ref, g, val * (1.0 / win) - tok, slice(0, t))
        if is_seq_start is not None:
            div = jnp.where(is_seq_start, cnt_ref[:, cols], float(win))
            pool_tail(p_ref, g, val[0:POOL_HALO] / div - tok[0:POOL_HALO], slice(0, POOL_HALO))

    def finish(x, out_rows):
        xbuf_ref[POOL_PAD:halo, :] = xbuf_ref[t + POOL_PAD:t + halo, :]
        y = jnp.dot(mix_ref[...], w_out_ref[...], preferred_element_type=F32)
        res = DEEPNORM_ALPHA * x + (1.0 + gate_ref[0]) * y
        o_ref[0, out_rows, :] = _layer_norm(res, lng_ref[...], lnb_ref[...])

    def half_step(h_next, p_into, p_from, x, out_rows, is_seq_start):
        pool_fill(p_from, is_seq_start)
        for k in range(n_pieces):
            project_piece(h_next, p_into, k)
            if k < GMLP_HEADS:
                gmlp_head(p_from, k)
            else:
                pool_group(p_from, k - GMLP_HEADS, is_seq_start)
        finish(x, out_rows)

    @pl.when(first_step)
    def _():
        for buf in (xbuf_ref, la_ref, lb_ref):
            buf[0:POOL_PAD, :] = jnp.zeros((POOL_PAD, buf.shape[1]), F32)
        h = modulate(xp_ref[0, 0:t, :], shc_ref[0], scc_ref[0])
        for k in range(n_pieces):
            project_piece(h, p0_ref, k)

    x0 = xp_ref[0, 0:t, :]
    x1 = xp_ref[0, t:2 * t, :]
    half_step(modulate(x1, shc_ref[0], scc_ref[0]), p1_ref, p0_ref, x0, slice(0, t), seq_start)
    half_step(modulate(xn_ref[0], shn_ref[0], scn_ref[0]), p0_ref, p1_ref, x1, slice(t, 2 * t), None)


def _layer0(x, shift, scale, gate, w_in, ng, nb, ws, bs, pw, pb, ps, w_out, lng, lnb):
    b, s, d = x.shape
    t = L0_TILE
    width = d
    hd = width // GMLP_HEADS
    gd = width // len(POOL_WINDOWS)
    bsf = jnp.broadcast_to(bs.T[:, :, None], (GMLP_BLOCK, GMLP_HEADS, hd)).reshape(GMLP_BLOCK, width)
    pos = np.arange(POOL_HALO, dtype=np.float32)[:, None] + 1.0
    cnt = np.concatenate(
        [np.broadcast_to(np.minimum(pos, float(w)), (POOL_HALO, gd)) for w in POOL_WINDOWS], axis=1)
    row = lambda a: a.reshape(1, -1)
    tiles = s // t

    def next_tile(i, j):
        flat = jnp.minimum(i * tiles + 2 * j + 2, b * tiles - 1)
        return flat // tiles, flat % tiles

    vec = pl.BlockSpec((1, 1, d), lambda i, j: (i, 0, 0))
    vec_next = pl.BlockSpec((1, 1, d), lambda i, j: (next_tile(i, j)[0], 0, 0))
    return pl.pallas_call(
        _l0_kernel,
        grid=(b, tiles // 2),
        in_specs=[
            pl.BlockSpec((1, 2 * t, d), lambda i, j: (i, j, 0)),
            pl.BlockSpec((1, t, d), lambda i, j: (*next_tile(i, j), 0)),
            vec, vec, vec_next, vec_next, vec,
            _resident(w_in.shape),
            _resident((1, hd)), _resident((1, hd)),
            _resident(ws.shape),
            _resident(bsf.shape),
            _resident(pw.shape),
            _resident((1, width)), _resident((1, width)),
            _resident(cnt.shape),
            _resident(w_out.shape),
            _resident((1, d)), _resident((1, d)),
        ],
        out_specs=pl.BlockSpec((1, 2 * t, d), lambda i, j: (i, j, 0)),
        out_shape=jax.ShapeDtypeStruct((b, s, d), F32),
        scratch_shapes=[
            pltpu.VMEM((t, w_in.shape[1]), F32),
            pltpu.VMEM((t, w_in.shape[1]), F32),
            pltpu.VMEM((POOL_PAD + POOL_HALO + t, width), F32),
            pltpu.VMEM((POOL_PAD + POOL_HALO + t, gd), F32),
            pltpu.VMEM((POOL_PAD + POOL_HALO + t, gd), F32),
            pltpu.VMEM((t, 2 * width), BF16),
        ],
        compiler_params=pltpu.CompilerParams(
            dimension_semantics=("arbitrary", "arbitrary"), vmem_limit_bytes=VMEM_LIMIT_BYTES),
        name="layer0_gmlp_pool",
    )(x, x, shift, scale, shift, scale, gate, w_in.astype(BF16), row(ng), row(nb), ws, bsf,
      pw.astype(BF16), row(pb), row(ps), jnp.asarray(cnt), w_out.astype(BF16), row(lng), row(lnb))


def _fold_kernel(wq_ref, wk_ref, o_ref):
    o_ref[0] = lax.dot_general(wq_ref[0], wk_ref[0], (((1,), (1,)), ((), ())),
                               precision=lax.Precision.HIGHEST, preferred_element_type=F32)


def _fold_query_key(w_uq, w_uk):
    qr = w_uq.shape[0]
    wq = jnp.transpose(w_uq[:, :, :MLA_NOPE], (1, 0, 2))
    wk = jnp.transpose(w_uk, (1, 0, 2))
    return pl.pallas_call(
        _fold_kernel,
        grid=(MLA_HEADS,),
        in_specs=[pl.BlockSpec((1, qr, MLA_NOPE), lambda h: (h, 0, 0)),
                  pl.BlockSpec((1, MLA_KV_RANK, MLA_NOPE), lambda h: (h, 0, 0))],
        out_specs=pl.BlockSpec((1, qr, MLA_KV_RANK), lambda h: (h, 0, 0)),
        out_shape=jax.ShapeDtypeStruct((MLA_HEADS, qr, MLA_KV_RANK), F32),
        name="fold_uq_uk",
    )(wq, wk)


def _rope_kernel(pos_ref, inv_ref, cos_ref, sin_ref):
    ang = inv_ref[...] * pos_ref[0].astype(F32)
    cos_ref[0] = jnp.cos(ang)
    sin_ref[0] = jnp.sin(ang)


def _rope_tables(positions):
    b, s = positions.shape
    half = MLA_ROPE // 2
    inv = (1.0 / (ROPE_THETA ** (np.arange(0, MLA_ROPE, 2, dtype=np.float64) / MLA_ROPE))).astype(np.float32)
    table = pl.BlockSpec((1, half, s), lambda i: (i, 0, 0))
    return pl.pallas_call(
        _rope_kernel,
        grid=(b,),
        in_specs=[pl.BlockSpec((1, 1, s), lambda i: (i, 0, 0)), _resident((half, 1))],
        out_specs=[table, table],
        out_shape=[jax.ShapeDtypeStruct((b, half, s), F32)] * 2,
        name="rope_tables",
    )(positions.reshape(b, 1, s), jnp.asarray(inv[:, None]))


def _l1_proj_kernel(x_ref, shift_ref, scale_ref, cos_ref, sin_ref, w_in_ref, qg_ref, kvg_ref, wq_ref,
                    q_ref, k_ref, vt_ref, zg_ref):
    t = x_ref.shape[1]
    qr, kvr = MLA_Q_RANK, MLA_KV_RANK
    h = (x_ref[0] * (1.0 + scale_ref[0]) + shift_ref[0]).astype(BF16)
    cs_t = jnp.concatenate([cos_ref[0], sin_ref[0], cos_ref[0], sin_ref[0]], axis=0)
    cs = cs_t.T

    lat = jnp.dot(h, w_in_ref[:, 0:qr + kvr + V7X_LANES], preferred_element_type=F32)
    q_c = lat[:, 0:qr]
    kv_c = lat[:, qr:qr + kvr]
    k_r = lat[:, qr + kvr:qr + kvr + V7X_LANES]

    q_c = q_c * lax.rsqrt(jnp.mean(q_c * q_c, axis=-1, keepdims=True) + LN_EPS) * qg_ref[...]
    kv_c = kv_c * lax.rsqrt(jnp.mean(kv_c * kv_c, axis=-1, keepdims=True) + LN_EPS) * kvg_ref[...]

    kt = k_r * cs
    quarter = V7X_LANES // 4
    lane = lax.broadcasted_iota(jnp.int32, kt.shape, 1)
    swapped = jnp.where((lane // quarter) % 2 == 0,
                        pltpu.roll(kt, V7X_LANES - quarter, 1), pltpu.roll(kt, quarter, 1))
    k_ref[0] = jnp.concatenate([kv_c, kt + swapped], axis=-1).astype(BF16)
    vt_ref[0, 0:kvr, :] = kv_c.T.astype(BF16)
    vt_ref[0, kvr:VT_ROWS, :] = jnp.ones((VT_ROWS - kvr, t), BF16)

    qn_t = (q_c * (ATTN_SCALE * LOG2E)).T.astype(BF16)

    def query_heads(g):
        per = MLA_HEADS // PROJ_PIECES
        qf_t = jnp.dot(wq_ref[g * per * HEAD_LANES:(g + 1) * per * HEAD_LANES, :], qn_t,
                       preferred_element_type=F32)
        for k in range(per):
            hh = g * per + k
            q_lat = qf_t[k * HEAD_LANES:k * HEAD_LANES + V7X_LANES]
            q_rope = qf_t[k * HEAD_LANES + V7X_LANES:(k + 1) * HEAD_LANES] * cs_t
            q_ref[0, 0, :, hh * t:(hh + 1) * t] = jnp.concatenate([q_lat, q_rope], axis=0).astype(BF16)

    def gate_cols(g):
        zw = zg_ref.shape[2] // PROJ_PIECES
        z0 = qr + kvr + V7X_LANES + g * zw
        z = jnp.dot(h, w_in_ref[:, z0:z0 + zw], preferred_element_type=F32)
        zg_ref[0, :, g * zw:(g + 1) * zw] = _silu(z).astype(BF16)

    for g in range(PROJ_PIECES):
        gate_cols(g)
        query_heads(g)


def _layer1_proj(x, shift, scale, cos_t, sin_t, w_in, qg, kvg, wq):
    b, s, d = x.shape
    t = Q_TILE
    zw = w_in.shape[1] - (MLA_Q_RANK + MLA_KV_RANK + V7X_LANES)
    vec = pl.BlockSpec((1, 1, d), lambda i, j: (i, 0, 0))
    return pl.pallas_call(
        _l1_proj_kernel,
        grid=(b, s // t),
        in_specs=[
            pl.BlockSpec((1, t, d), lambda i, j: (i, j, 0)),
            vec, vec,
            pl.BlockSpec((1, MLA_ROPE // 2, t), lambda i, j: (i, 0, j)),
            pl.BlockSpec((1, MLA_ROPE // 2, t), lambda i, j: (i, 0, j)),
            _resident(w_in.shape),
            _resident((1, MLA_Q_RANK)), _resident((1, MLA_KV_RANK)),
            _resident(wq.shape),
        ],
        out_specs=[
            pl.BlockSpec((1, 1, HEAD_LANES, MLA_HEADS * t), lambda i, j: (i, j, 0, 0)),
            pl.BlockSpec((1, t, HEAD_LANES), lambda i, j: (i, j, 0)),
            pl.BlockSpec((1, VT_ROWS, t), lambda i, j: (i, 0, j)),
            pl.BlockSpec((1, t, zw), lambda i, j: (i, j, 0)),
        ],
        out_shape=[
            jax.ShapeDtypeStruct((b, s // t, HEAD_LANES, MLA_HEADS * t), BF16),
            jax.ShapeDtypeStruct((b, s, HEAD_LANES), BF16),
            jax.ShapeDtypeStruct((b, VT_ROWS, s), BF16),
            jax.ShapeDtypeStruct((b, s, zw), BF16),
        ],
        compiler_params=pltpu.CompilerParams(
            dimension_semantics=("arbitrary", "arbitrary"), vmem_limit_bytes=VMEM_LIMIT_BYTES),
        name="layer1_proj",
    )(x, shift, scale, cos_t, sin_t, w_in, qg.reshape(1, -1), kvg.reshape(1, -1), wq)


def _attn_kernel(q_ref, k_ref, vt_ref, zg_ref, x_ref, gate_ref, wuv_ref, w_out_ref, lng_ref, lnb_ref,
                 o_ref, s0_ref, s1_ref, mc0_ref, mc1_ref, m_ref, acc_ref, mix_ref):
    qt = x_ref.shape[1]
    rows = q_ref.shape[3]
    qi = pl.program_id(1)

    group = rows // ATTN_COL_GROUPS

    def scores(start, s_ref, mc_ref, masked, g):
        start = pl.multiple_of(start, qt)
        cols = slice(g * group, (g + 1) * group)
        st = jnp.dot(k_ref[0, pl.ds(start, qt), :], q_ref[0, 0, :, cols], preferred_element_type=F32)
        if masked:
            k_chunk = lax.broadcasted_iota(jnp.int32, (qt, group), 0) // CHUNK
            q_chunk = (lax.broadcasted_iota(jnp.int32, (qt, group), 1) % qt) // CHUNK
            st = jnp.where(k_chunk <= q_chunk, st, MASK_VALUE)
        s_ref[:, cols] = st
        mc_ref[:, cols] = jnp.max(st, axis=0, keepdims=True)

    def accumulate(start, s_ref, mc_ref, g):
        start = pl.multiple_of(start, qt)
        cols = slice(g * group, (g + 1) * group)
        m_prev = m_ref[:, cols]
        m_new = jnp.maximum(m_prev, mc_ref[:, cols])
        alpha = jnp.exp2(m_prev - m_new)
        pt = jnp.exp2(s_ref[:, cols] - m_new).astype(BF16)
        pv = jnp.dot(vt_ref[0, :, pl.ds(start, qt)], pt, preferred_element_type=F32)
        acc_ref[:, cols] = alpha * acc_ref[:, cols] + pv
        m_ref[:, cols] = m_new

    m_ref[...] = jnp.full(m_ref.shape, MASK_VALUE, F32)
    acc_ref[...] = jnp.zeros(acc_ref.shape, F32)

    slot0, slot1 = (s0_ref, mc0_ref), (s1_ref, mc1_ref)

    def folded_start(i):
        return jnp.where(i == 0, qi, i - 1) * qt

    def trip(i, into, fold):
        for g in range(ATTN_COL_GROUPS):
            scores(i * qt, *into, False, g)
            accumulate(folded_start(i), *fold, g)

    def own_scores(slot):
        for g in range(ATTN_COL_GROUPS):
            scores(qi * qt, *slot, True, g)

    odd = qi % 2

    @pl.when(odd == 1)
    def _():
        own_scores(slot1)
        trip(0, slot0, slot1)

    @pl.when(odd == 0)
    def _():
        own_scores(slot0)

    def pair(p, carry):
        trip(2 * p + odd, slot1, slot0)
        trip(2 * p + odd + 1, slot0, slot1)
        return carry

    lax.fori_loop(0, qi // 2, pair, 0)

    assert ATTN_COL_GROUPS == MLA_HEADS

    def gated_value(hh):
        hcols = slice(hh * qt, (hh + 1) * qt)
        inv_l = 1.0 / acc_ref[MLA_KV_RANK:MLA_KV_RANK + 1, hcols]
        o_t = (acc_ref[0:MLA_KV_RANK, hcols] * inv_l).astype(BF16)
        o_h = lax.dot_general(o_t, wuv_ref[hh], (((0,), (0,)), ((), ())),
                              preferred_element_type=F32)
        cols = slice(hh * MLA_V, (hh + 1) * MLA_V)
        mix_ref[:, cols] = (o_h * zg_ref[0, :, cols].astype(F32)).astype(BF16)

    for hh in range(MLA_HEADS + EPILOGUE_LAG):
        if hh < MLA_HEADS:
            accumulate(folded_start(qi), *slot0, hh)
        if hh >= EPILOGUE_LAG:
            gated_value(hh - EPILOGUE_LAG)
    y = jnp.dot(mix_ref[...], w_out_ref[...], preferred_element_type=F32)
    res = DEEPNORM_ALPHA * x_ref[0] + (1.0 + gate_ref[0]) * y
    o_ref[0] = _layer_norm(res, lng_ref[...], lnb_ref[...])


def _layer1_attn(q, k, vt, zg, x, gate, wuv, w_out, lng, lnb):
    b, s, d = x.shape
    qt = Q_TILE
    rows = MLA_HEADS * qt
    width = zg.shape[2]
    return pl.pallas_call(
        _attn_kernel,
        grid=(b, s // qt),
        in_specs=[
            pl.BlockSpec((1, 1, HEAD_LANES, rows), lambda i, j: (i, j, 0, 0)),
            pl.BlockSpec((1, s, HEAD_LANES), lambda i, j: (i, 0, 0)),
            pl.BlockSpec((1, VT_ROWS, s), lambda i, j: (i, 0, 0)),
            pl.BlockSpec((1, qt, width), lambda i, j: (i, j, 0)),
            pl.BlockSpec((1, qt, d), lambda i, j: (i, j, 0)),
            pl.BlockSpec((1, 1, d), lambda i, j: (i, 0, 0)),
            _resident(wuv.shape),
            _resident(w_out.shape),
            _resident((1, d)), _resident((1, d)),
        ],
        out_specs=pl.BlockSpec((1, qt, d), lambda i, j: (i, j, 0)),
        out_shape=jax.ShapeDtypeStruct((b, s, d), F32),
        scratch_shapes=[
            pltpu.VMEM((qt, rows), F32),
            pltpu.VMEM((qt, rows), F32),
            pltpu.VMEM((1, rows), F32),
            pltpu.VMEM((1, rows), F32),
            pltpu.VMEM((1, rows), F32),
            pltpu.VMEM((VT_ROWS, rows), F32),
            pltpu.VMEM((qt, width), BF16),
        ],
        compiler_params=pltpu.CompilerParams(
            dimension_semantics=("arbitrary", "arbitrary"), vmem_limit_bytes=VMEM_LIMIT_BYTES),
        name="layer1_attn",
    )(q, k, vt, zg, x, gate, wuv, w_out, lng.reshape(1, -1), lnb.reshape(1, -1))


def _layer1_weights(w_in, w_uq, w_uk, w_uv):
    qr, kvr, half = MLA_Q_RANK, MLA_KV_RANK, MLA_ROPE // 2
    k1 = w_in[:, qr + kvr:qr + kvr + half]
    k2 = w_in[:, qr + kvr + half:qr + kvr + MLA_ROPE]
    w_in_p = jnp.concatenate(
        [w_in[:, :qr + kvr], k1, -k2, k2, k1, w_in[:, qr + kvr + MLA_ROPE:]], axis=1).astype(BF16)
    w_lat = jnp.transpose(_fold_query_key(w_uq, w_uk), (1, 0, 2))
    r1 = w_uq[:, :, MLA_NOPE:MLA_NOPE + half]
    r2 = w_uq[:, :, MLA_NOPE + half:]
    wq = jnp.concatenate([w_lat, r1, -r2, r2, r1], axis=2).reshape(qr, MLA_HEADS * HEAD_LANES).T.astype(BF16)
    wuv = jnp.transpose(w_uv, (1, 0, 2)).astype(BF16)
    return w_in_p, wq, wuv


def kernel(x, c, positions, ada_w, ada_b, ln_g, ln_b, e_w_in, gmlp_norm_g, gmlp_norm_b, gmlp_ws,
           gmlp_bs, pool_w, pool_b, pool_scale, e_w_out, o_w_in, mla_q_norm_g, mla_kv_norm_g,
           mla_w_uq, mla_w_uk, mla_w_uv, o_w_out):
    b, s, d = x.shape
    assert s % (2 * L0_TILE) == 0 and s % Q_TILE == 0 and Q_TILE % CHUNK == 0
    assert L0_TILE % GMLP_BLOCK == 0 and ada_w.shape[0] == DEPTH

    mod = _modulation(c, ada_w, ada_b)
    shift, scale, gate = (mod[:, :, i * d:(i + 1) * d].reshape(DEPTH, b, 1, d) for i in range(3))

    x = _layer0(x, shift[0], scale[0], gate[0], e_w_in[0], gmlp_norm_g[0], gmlp_norm_b[0], gmlp_ws[0],
                gmlp_bs[0], pool_w[0], pool_b[0], pool_scale[0], e_w_out[0], ln_g[0], ln_b[0])

    w_in_p, wq, wuv = _layer1_weights(o_w_in[0], mla_w_uq[0], mla_w_uk[0], mla_w_uv[0])
    cos_t, sin_t = _rope_tables(positions)
    q, k, vt, zg = _layer1_proj(x, shift[1], scale[1], cos_t, sin_t, w_in_p, mla_q_norm_g[0], mla_kv_norm_g[0], wq)
    return _layer1_attn(q, k, vt, zg, x, gate[1], wuv, o_w_out[0].astype(BF16), ln_g[1], ln_b[1])
```

```python
import functools
import math

import numpy as np
import jax
import jax.numpy as jnp
from jax import lax
from jax.experimental import pallas as pl
from jax.experimental.pallas import tpu as pltpu

F32 = jnp.float32
BF16 = jnp.bfloat16

CHUNK = 64
LN_EPS = 1e-5
GMLP_HEADS = 4
GMLP_BLOCK = 128
POOL_WINDOWS = (2, 4, 8, 16)
MLA_HEADS = 16
MLA_NOPE = 128
MLA_ROPE = 64
MLA_V = 128
MLA_Q_RANK = 256
MLA_KV_RANK = 128
ROPE_THETA = 10000.0
ATTN_SCALE = (MLA_NOPE + MLA_ROPE) ** -0.5
DEPTH = 2
DEEPNORM_ALPHA = (2.0 * DEPTH) ** 0.25

V7X_LANES = 128
V7X_MXU_WIDTH = 256
V7X_VMEM_BYTES = 64 * 1024 * 1024
VMEM_LIMIT_BYTES = V7X_VMEM_BYTES - 8 * 1024 * 1024

POOL_HALO = max(POOL_WINDOWS)
V7X_SUBLANES = 8
POOL_PAD = V7X_SUBLANES
HEAD_LANES = 2 * V7X_LANES
MASK_VALUE = -1e30
LOG2E = math.log2(math.e)

L0_TILE = 256
L0_STEP_TILES = 4
PROJ_STEP_TILES = 2
FOLD_STEP_HEADS = 4
Q_TILE = 256
ATTN_COL_GROUPS = MLA_HEADS
EPILOGUE_LAG = 4
PROJ_PIECES = 4
BF16_SUBLANES = 16
VT_ROWS = MLA_KV_RANK + BF16_SUBLANES


def _silu(v):
    return v * jax.nn.sigmoid(v)


def _layer_norm(v, g, b):
    mu = jnp.mean(v, axis=-1, keepdims=True)
    d = v - mu
    var = jnp.mean(d * d, axis=-1, keepdims=True)
    return d * lax.rsqrt(var + LN_EPS) * g + b


def _resident(shape):
    nd = len(shape)
    return pl.BlockSpec(shape, lambda *_: (0,) * nd, pipeline_mode=pl.Buffered(1))


def _mod_kernel(c_ref, w_ref, b_ref, o_ref):
    cond = _silu(c_ref[...]).astype(BF16)
    o_ref[0] = jnp.dot(cond, w_ref[0].astype(BF16), preferred_element_type=F32) + b_ref[0]


def _modulation(c, ada_w, ada_b):
    depth, d, n = ada_w.shape
    b = c.shape[0]
    tn = d
    return pl.pallas_call(
        _mod_kernel,
        grid=(depth, n // tn),
        in_specs=[
            pl.BlockSpec((b, d), lambda l, j: (0, 0)),
            pl.BlockSpec((1, d, tn), lambda l, j: (l, 0, j)),
            pl.BlockSpec((1, 1, tn), lambda l, j: (l, 0, j)),
        ],
        out_specs=pl.BlockSpec((1, b, tn), lambda l, j: (l, 0, j)),
        out_shape=jax.ShapeDtypeStruct((depth, b, n), F32),
        compiler_params=pltpu.CompilerParams(
            dimension_semantics=("arbitrary", "arbitrary"), vmem_limit_bytes=VMEM_LIMIT_BYTES),
        name="adaln_mod",
    )(c, ada_w, ada_b.reshape(depth, 1, n))


def _l0_kernel(xp_ref, xn_ref, shc_ref, scc_ref, shn_ref, scn_ref, gate_ref, w_in_ref, ng_ref, nb_ref,
               ws_ref, bsf_ref, pw_ref, pb_ref, ps_ref, cnt_ref, w_out_ref, lng_ref, lnb_ref,
               o_ref, p0_ref, p1_ref, xbuf_ref, la_ref, lb_ref, mix_ref):
    t, d = xn_ref.shape[1], xn_ref.shape[2]
    width = d
    hd = width // GMLP_HEADS
    gd = width // len(POOL_WINDOWS)
    halo = POOL_PAD + POOL_HALO
    first_step = jnp.logical_and(pl.program_id(0) == 0, pl.program_id(1) == 0)
    seq_start = pl.program_id(1) == 0
    n_pieces = GMLP_HEADS + len(POOL_WINDOWS)
    mxu_cols = w_in_ref.shape[1] // V7X_MXU_WIDTH
    cuts = [V7X_MXU_WIDTH * ((k * mxu_cols) // n_pieces) for k in range(n_pieces + 1)]

    def modulate(x, shift, scale):
        return (x * (1.0 + scale) + shift).astype(BF16)

    def project_piece(h, p_ref, k):
        cols = slice(cuts[k], cuts[k + 1])
        p_ref[:, cols] = jnp.dot(h, w_in_ref[:, cols], preferred_element_type=F32)

    r = lax.broadcasted_iota(jnp.int32, (GMLP_BLOCK, GMLP_BLOCK), 0) // CHUNK
    c = lax.broadcasted_iota(jnp.int32, (GMLP_BLOCK, GMLP_BLOCK), 1) // CHUNK
    tri = c <= r

    def gmlp_head(p_ref, hh):
        cols = slice(hh * hd, (hh + 1) * hd)
        vn = _layer_norm(p_ref[:, width + hh * hd:width + (hh + 1) * hd],
                         ng_ref[...], nb_ref[...]).astype(BF16)
        w = jnp.where(tri, ws_ref[hh], 0.0).astype(BF16)
        for k in range(t // GMLP_BLOCK):
            rows = slice(k * GMLP_BLOCK, (k + 1) * GMLP_BLOCK)
            sv = jnp.dot(w, vn[rows], preferred_element_type=F32) + bsf_ref[:, cols]
            za = p_ref[rows, 2 * width + hh * hd:2 * width + (hh + 1) * hd]
            mix_ref[rows, cols] = (p_ref[rows, cols] * sv * _silu(za)).astype(BF16)

    def pool_fill(p_ref, is_seq_start):
        carry = xbuf_ref[POOL_PAD:halo, :]
        if is_seq_start is not None:
            carry = jnp.where(is_seq_start, 0.0, carry)
            xbuf_ref[POOL_PAD:halo, :] = carry
        xbuf_ref[halo:halo + t, :] = p_ref[:, 3 * width:4 * width]

    def pool_tail(p_ref, g, pooled, rows):
        cols = slice(g * gd, (g + 1) * gd)
        y = jnp.dot(pooled.astype(BF16), pw_ref[g], preferred_element_type=F32)
        zb = p_ref[rows, 4 * width + g * gd:4 * width + (g + 1) * gd]
        mix_ref[rows, width + g * gd:width + (g + 1) * gd] = (
            (y + pb_ref[:, cols]) * ps_ref[:, cols] * _silu(zb)).astype(BF16)

    def pool_group(p_ref, g, is_seq_start):
        win = POOL_WINDOWS[g]
        cols = slice(g * gd, (g + 1) * gd)
        levels = win.bit_length() - 1
        src = xbuf_ref.at[:, cols]
        for lvl in range(1, levels + 1):
            shift = 1 << (lvl - 1)
            lo = halo if lvl == levels else POOL_PAD
            n = halo + t - lo
            val = src[lo:lo + n, :] + src[lo - shift:lo - shift + n, :]
            if lvl < levels:
                dst = la_ref if lvl % 2 else lb_ref
                dst[lo:lo + n, :] = val
                src = dst
        tok = xbuf_ref[halo:halo + t, cols]
        pool_tail(p_ref, g, val * (1.0 / win) - tok, slice(0, t))
        if is_seq_start is not None:
            div = jnp.where(is_seq_start, cnt_ref[:, cols], float(win))
            pool_tail(p_ref, g, val[0:POOL_HALO] / div - tok[0:POOL_HALO], slice(0, POOL_HALO))

    def finish(x, out_rows):
        xbuf_ref[POOL_PAD:halo, :] = xbuf_ref[t + POOL_PAD:t + halo, :]
        y = jnp.dot(mix_ref[...], w_out_ref[...], preferred_element_type=F32)
        res = DEEPNORM_ALPHA * x + (1.0 + gate_ref[0]) * y
        o_ref[0, out_rows, :] = _layer_norm(res, lng_ref[...], lnb_ref[...])

    def half_step(h_next, p_into, p_from, x, out_rows, is_seq_start):
        pool_fill(p_from, is_seq_start)
        for k in range(n_pieces):
            project_piece(h_next, p_into, k)
            if k < GMLP_HEADS:
                gmlp_head(p_from, k)
            else:
                pool_group(p_from, k - GMLP_HEADS, is_seq_start)
        finish(x, out_rows)

    @pl.when(first_step)
    def _():
        for buf in (xbuf_ref, la_ref, lb_ref):
            buf[0:POOL_PAD, :] = jnp.zeros((POOL_PAD, buf.shape[1]), F32)
        h = modulate(xp_ref[0, 0:t, :], shc_ref[0], scc_ref[0])
        for k in range(n_pieces):
            project_piece(h, p0_ref, k)

    slots = (p0_ref, p1_ref)
    for i in range(L0_STEP_TILES):
        rows = slice(i * t, (i + 1) * t)
        if i + 1 < L0_STEP_TILES:
            h_next = modulate(xp_ref[0, (i + 1) * t:(i + 2) * t, :], shc_ref[0], scc_ref[0])
        else:
            h_next = modulate(xn_ref[0], shn_ref[0], scn_ref[0])
        half_step(h_next, slots[(i + 1) % 2], slots[i % 2], xp_ref[0, rows, :], rows,
                  seq_start if i == 0 else None)


def _layer0(x, shift, scale, gate, w_in, ng, nb, ws, bs, pw, pb, ps, w_out, lng, lnb):
    b, s, d = x.shape
    t = L0_TILE
    width = d
    hd = width // GMLP_HEADS
    gd = width // len(POOL_WINDOWS)
    bsf = jnp.broadcast_to(bs.T[:, :, None], (GMLP_BLOCK, GMLP_HEADS, hd)).reshape(GMLP_BLOCK, width)
    pos = np.arange(POOL_HALO, dtype=np.float32)[:, None] + 1.0
    cnt = np.concatenate(
        [np.broadcast_to(np.minimum(pos, float(w)), (POOL_HALO, gd)) for w in POOL_WINDOWS], axis=1)
    row = lambda a: a.reshape(1, -1)
    tiles = s // t
    per_step = L0_STEP_TILES

    def next_tile(i, j):
        flat = jnp.minimum(i * tiles + per_step * (j + 1), b * tiles - 1)
        return flat // tiles, flat % tiles

    vec = pl.BlockSpec((1, 1, d), lambda i, j: (i, 0, 0))
    vec_next = pl.BlockSpec((1, 1, d), lambda i, j: (next_tile(i, j)[0], 0, 0))
    return pl.pallas_call(
        _l0_kernel,
        grid=(b, tiles // per_step),
        in_specs=[
            pl.BlockSpec((1, per_step * t, d), lambda i, j: (i, j, 0)),
            pl.BlockSpec((1, t, d), lambda i, j: (*next_tile(i, j), 0)),
            vec, vec, vec_next, vec_next, vec,
            _resident(w_in.shape),
            _resident((1, hd)), _resident((1, hd)),
            _resident(ws.shape),
            _resident(bsf.shape),
            _resident(pw.shape),
            _resident((1, width)), _resident((1, width)),
            _resident(cnt.shape),
            _resident(w_out.shape),
            _resident((1, d)), _resident((1, d)),
        ],
        out_specs=pl.BlockSpec((1, per_step * t, d), lambda i, j: (i, j, 0)),
        out_shape=jax.ShapeDtypeStruct((b, s, d), F32),
        scratch_shapes=[
            pltpu.VMEM((t, w_in.shape[1]), F32),
            pltpu.VMEM((t, w_in.shape[1]), F32),
            pltpu.VMEM((POOL_PAD + POOL_HALO + t, width), F32),
            pltpu.VMEM((POOL_PAD + POOL_HALO + t, gd), F32),
            pltpu.VMEM((POOL_PAD + POOL_HALO + t, gd), F32),
            pltpu.VMEM((t, 2 * width), BF16),
        ],
        compiler_params=pltpu.CompilerParams(
            dimension_semantics=("arbitrary", "arbitrary"), vmem_limit_bytes=VMEM_LIMIT_BYTES),
        name="layer0_gmlp_pool",
    )(x, x, shift, scale, shift, scale, gate, w_in.astype(BF16), row(ng), row(nb), ws, bsf,
      pw.astype(BF16), row(pb), row(ps), jnp.asarray(cnt), w_out.astype(BF16), row(lng), row(lnb))


def _fold_kernel(wq_ref, wk_ref, o_ref):
    for hh in range(wq_ref.shape[0]):
        o_ref[hh] = lax.dot_general(wq_ref[hh], wk_ref[hh], (((1,), (1,)), ((), ())),
                                    precision=lax.Precision.HIGHEST, preferred_element_type=F32)


def _fold_query_key(w_uq, w_uk):
    qr = w_uq.shape[0]
    wq = jnp.transpose(w_uq[:, :, :MLA_NOPE], (1, 0, 2))
    wk = jnp.transpose(w_uk, (1, 0, 2))
    return pl.pallas_call(
        _fold_kernel,
        grid=(MLA_HEADS // FOLD_STEP_HEADS,),
        in_specs=[pl.BlockSpec((FOLD_STEP_HEADS, qr, MLA_NOPE), lambda h: (h, 0, 0)),
                  pl.BlockSpec((FOLD_STEP_HEADS, MLA_KV_RANK, MLA_NOPE), lambda h: (h, 0, 0))],
        out_specs=pl.BlockSpec((FOLD_STEP_HEADS, qr, MLA_KV_RANK), lambda h: (h, 0, 0)),
        out_shape=jax.ShapeDtypeStruct((MLA_HEADS, qr, MLA_KV_RANK), F32),
        name="fold_uq_uk",
    )(wq, wk)


def _rope_kernel(pos_ref, inv_ref, cos_ref, sin_ref):
    ang = inv_ref[...] * pos_ref[0].astype(F32)
    cos_ref[0] = jnp.cos(ang)
    sin_ref[0] = jnp.sin(ang)


def _rope_tables(positions):
    b, s = positions.shape
    half = MLA_ROPE // 2
    inv = (1.0 / (ROPE_THETA ** (np.arange(0, MLA_ROPE, 2, dtype=np.float64) / MLA_ROPE))).astype(np.float32)
    table = pl.BlockSpec((1, half, s), lambda i: (i, 0, 0))
    return pl.pallas_call(
        _rope_kernel,
        grid=(b,),
        in_specs=[pl.BlockSpec((1, 1, s), lambda i: (i, 0, 0)), _resident((half, 1))],
        out_specs=[table, table],
        out_shape=[jax.ShapeDtypeStruct((b, half, s), F32)] * 2,
        name="rope_tables",
    )(positions.reshape(b, 1, s), jnp.asarray(inv[:, None]))


def _l1_proj_kernel(x_ref, shift_ref, scale_ref, cos_ref, sin_ref, w_in_ref, qg_ref, kvg_ref, wq_ref,
                    q_ref, k_ref, vt_ref, zg_ref):
    t = x_ref.shape[1] // PROJ_STEP_TILES
    qr, kvr = MLA_Q_RANK, MLA_KV_RANK
    for i in range(PROJ_STEP_TILES):
        rows = slice(i * t, (i + 1) * t)
        _project_tile(x_ref[0, rows, :], shift_ref[0], scale_ref[0], cos_ref[0, :, rows], sin_ref[0, :, rows],
                      w_in_ref, qg_ref, kvg_ref, wq_ref, q_ref.at[0, i], k_ref.at[0, rows, :],
                      vt_ref.at[0, :, rows], zg_ref.at[0, rows, :], t, qr, kvr)


def _project_tile(x, shift, scale, cos_t, sin_t, w_in_ref, qg_ref, kvg_ref, wq_ref,
                  q_ref, k_ref, vt_ref, zg_ref, t, qr, kvr):
    h = (x * (1.0 + scale) + shift).astype(BF16)
    cs_t = jnp.concatenate([cos_t, sin_t, cos_t, sin_t], axis=0)
    cs = cs_t.T

    lat = jnp.dot(h, w_in_ref[:, 0:qr + kvr + V7X_LANES], preferred_element_type=F32)
    q_c = lat[:, 0:qr]
    kv_c = lat[:, qr:qr + kvr]
    k_r = lat[:, qr + kvr:qr + kvr + V7X_LANES]

    q_c = q_c * lax.rsqrt(jnp.mean(q_c * q_c, axis=-1, keepdims=True) + LN_EPS) * qg_ref[...]
    kv_c = kv_c * lax.rsqrt(jnp.mean(kv_c * kv_c, axis=-1, keepdims=True) + LN_EPS) * kvg_ref[...]

    kt = k_r * cs
    quarter = V7X_LANES // 4
    lane = lax.broadcasted_iota(jnp.int32, kt.shape, 1)
    swapped = jnp.where((lane // quarter) % 2 == 0,
                        pltpu.roll(kt, V7X_LANES - quarter, 1), pltpu.roll(kt, quarter, 1))
    k_ref[...] = jnp.concatenate([kv_c, kt + swapped], axis=-1).astype(BF16)
    vt_ref[0:kvr, :] = kv_c.T.astype(BF16)
    vt_ref[kvr:VT_ROWS, :] = jnp.ones((VT_ROWS - kvr, t), BF16)

    qn_t = (q_c * (ATTN_SCALE * LOG2E)).T.astype(BF16)

    def query_heads(g):
        per = MLA_HEADS // PROJ_PIECES
        qf_t = jnp.dot(wq_ref[g * per * HEAD_LANES:(g + 1) * per * HEAD_LANES, :], qn_t,
                       preferred_element_type=F32)
        for k in range(per):
            hh = g * per + k
            q_lat = qf_t[k * HEAD_LANES:k * HEAD_LANES + V7X_LANES]
            q_rope = qf_t[k * HEAD_LANES + V7X_LANES:(k + 1) * HEAD_LANES] * cs_t
            q_ref[hh] = jnp.concatenate([q_lat, q_rope], axis=0).astype(BF16)

    def gate_cols(g):
        zw = zg_ref.shape[1] // PROJ_PIECES
        z0 = qr + kvr + V7X_LANES + g * zw
        z = jnp.dot(h, w_in_ref[:, z0:z0 + zw], preferred_element_type=F32)
        zg_ref[:, g * zw:(g + 1) * zw] = _silu(z).astype(BF16)

    for g in range(PROJ_PIECES):
        gate_cols(g)
        query_heads(g)


def _layer1_proj(x, shift, scale, cos_t, sin_t, w_in, qg, kvg, wq):
    b, s, d = x.shape
    t = Q_TILE
    n = PROJ_STEP_TILES
    zw = w_in.shape[1] - (MLA_Q_RANK + MLA_KV_RANK + V7X_LANES)
    vec = pl.BlockSpec((1, 1, d), lambda i, j: (i, 0, 0))
    return pl.pallas_call(
        _l1_proj_kernel,
        grid=(b, s // (n * t)),
        in_specs=[
            pl.BlockSpec((1, n * t, d), lambda i, j: (i, j, 0)),
            vec, vec,
            pl.BlockSpec((1, MLA_ROPE // 2, n * t), lambda i, j: (i, 0, j)),
            pl.BlockSpec((1, MLA_ROPE // 2, n * t), lambda i, j: (i, 0, j)),
            _resident(w_in.shape),
            _resident((1, MLA_Q_RANK)), _resident((1, MLA_KV_RANK)),
            _resident(wq.shape),
        ],
        out_specs=[
            pl.BlockSpec((1, n, MLA_HEADS, HEAD_LANES, t), lambda i, j: (i, j, 0, 0, 0)),
            pl.BlockSpec((1, n * t, HEAD_LANES), lambda i, j: (i, j, 0)),
            pl.BlockSpec((1, VT_ROWS, n * t), lambda i, j: (i, 0, j)),
            pl.BlockSpec((1, n * t, zw), lambda i, j: (i, j, 0)),
        ],
        out_shape=[
            jax.ShapeDtypeStruct((b, s // t, MLA_HEADS, HEAD_LANES, t), BF16),
            jax.ShapeDtypeStruct((b, s, HEAD_LANES), BF16),
            jax.ShapeDtypeStruct((b, VT_ROWS, s), BF16),
            jax.ShapeDtypeStruct((b, s, zw), BF16),
        ],
        compiler_params=pltpu.CompilerParams(
            dimension_semantics=("arbitrary", "arbitrary"), vmem_limit_bytes=VMEM_LIMIT_BYTES),
        name="layer1_proj",
    )(x, shift, scale, cos_t, sin_t, w_in, qg.reshape(1, -1), kvg.reshape(1, -1), wq)


def _attn_kernel(q_ref, k_ref, vt_ref, zg_ref, x_ref, gate_ref, wuv_ref, w_out_ref, lng_ref, lnb_ref,
                 o_ref, s0_ref, s1_ref, mc0_ref, mc1_ref, m_ref, acc_ref, mix_ref):
    qt = x_ref.shape[1]
    qi = pl.program_id(1)

    def scores(start, s_ref, mc_ref, masked, g):
        start = pl.multiple_of(start, qt)
        st = jnp.dot(k_ref[0, pl.ds(start, qt), :], q_ref[0, 0, g], preferred_element_type=F32)
        if masked:
            k_chunk = lax.broadcasted_iota(jnp.int32, (qt, qt), 0) // CHUNK
            q_chunk = lax.broadcasted_iota(jnp.int32, (qt, qt), 1) // CHUNK
            st = jnp.where(k_chunk <= q_chunk, st, MASK_VALUE)
        s_ref[g] = st
        mc_ref[g] = jnp.max(st, axis=0, keepdims=True)

    def accumulate(start, s_ref, mc_ref, g):
        start = pl.multiple_of(start, qt)
        m_prev = m_ref[g]
        m_new = jnp.maximum(m_prev, mc_ref[g])
        alpha = jnp.exp2(m_prev - m_new)
        pt = jnp.exp2(s_ref[g] - m_new).astype(BF16)
        pv = jnp.dot(vt_ref[0, :, pl.ds(start, qt)], pt, preferred_element_type=F32)
        acc_ref[g] = alpha * acc_ref[g] + pv
        m_ref[g] = m_new

    m_ref[...] = jnp.full(m_ref.shape, MASK_VALUE, F32)
    acc_ref[...] = jnp.zeros(acc_ref.shape, F32)

    slot0, slot1 = (s0_ref, mc0_ref), (s1_ref, mc1_ref)

    def folded_start(i):
        return jnp.where(i == 0, qi, i - 1) * qt

    def trip(i, into, fold):
        for g in range(ATTN_COL_GROUPS):
            scores(i * qt, *into, False, g)
            accumulate(folded_start(i), *fold, g)

    def own_scores(slot):
        for g in range(ATTN_COL_GROUPS):
            scores(qi * qt, *slot, True, g)

    odd = qi % 2

    @pl.when(odd == 1)
    def _():
        own_scores(slot1)
        trip(0, slot0, slot1)

    @pl.when(odd == 0)
    def _():
        own_scores(slot0)

    def pair(p, carry):
        trip(2 * p + odd, slot1, slot0)
        trip(2 * p + odd + 1, slot0, slot1)
        return carry

    lax.fori_loop(0, qi // 2, pair, 0)

    assert ATTN_COL_GROUPS == MLA_HEADS

    def gated_value(hh):
        inv_l = 1.0 / acc_ref[hh, MLA_KV_RANK:MLA_KV_RANK + 1, :]
        o_t = (acc_ref[hh, 0:MLA_KV_RANK, :] * inv_l).astype(BF16)
        o_h = lax.dot_general(o_t, wuv_ref[hh], (((0,), (0,)), ((), ())),
                              preferred_element_type=F32)
        cols = slice(hh * MLA_V, (hh + 1) * MLA_V)
        mix_ref[:, cols] = (o_h * zg_ref[0, :, cols].astype(F32)).astype(BF16)

    for hh in range(MLA_HEADS + EPILOGUE_LAG):
        if hh < MLA_HEADS:
            accumulate(folded_start(qi), *slot0, hh)
        if hh >= EPILOGUE_LAG:
            gated_value(hh - EPILOGUE_LAG)
    y = jnp.dot(mix_ref[...], w_out_ref[...], preferred_element_type=F32)
    res = DEEPNORM_ALPHA * x_ref[0] + (1.0 + gate_ref[0]) * y
    o_ref[0] = _layer_norm(res, lng_ref[...], lnb_ref[...])


def _layer1_attn(q, k, vt, zg, x, gate, wuv, w_out, lng, lnb):
    b, s, d = x.shape
    qt = Q_TILE
    width = zg.shape[2]
    return pl.pallas_call(
        _attn_kernel,
        grid=(b, s // qt),
        in_specs=[
            pl.BlockSpec((1, 1, MLA_HEADS, HEAD_LANES, qt), lambda i, j: (i, j, 0, 0, 0)),
            pl.BlockSpec((1, s, HEAD_LANES), lambda i, j: (i, 0, 0)),
            pl.BlockSpec((1, VT_ROWS, s), lambda i, j: (i, 0, 0)),
            pl.BlockSpec((1, qt, width), lambda i, j: (i, j, 0)),
            pl.BlockSpec((1, qt, d), lambda i, j: (i, j, 0)),
            pl.BlockSpec((1, 1, d), lambda i, j: (i, 0, 0)),
            _resident(wuv.shape),
            _resident(w_out.shape),
            _resident((1, d)), _resident((1, d)),
        ],
        out_specs=pl.BlockSpec((1, qt, d), lambda i, j: (i, j, 0)),
        out_shape=jax.ShapeDtypeStruct((b, s, d), F32),
        scratch_shapes=[
            pltpu.VMEM((MLA_HEADS, qt, qt), F32),
            pltpu.VMEM((MLA_HEADS, qt, qt), F32),
            pltpu.VMEM((MLA_HEADS, 1, qt), F32),
            pltpu.VMEM((MLA_HEADS, 1, qt), F32),
            pltpu.VMEM((MLA_HEADS, 1, qt), F32),
            pltpu.VMEM((MLA_HEADS, VT_ROWS, qt), F32),
            pltpu.VMEM((qt, width), BF16),
        ],
        compiler_params=pltpu.CompilerParams(
            dimension_semantics=("arbitrary", "arbitrary"), vmem_limit_bytes=VMEM_LIMIT_BYTES),
        name="layer1_attn",
    )(q, k, vt, zg, x, gate, wuv, w_out, lng.reshape(1, -1), lnb.reshape(1, -1))


def _layer1_weights(w_in, w_uq, w_uk, w_uv):
    qr, kvr, half = MLA_Q_RANK, MLA_KV_RANK, MLA_ROPE // 2
    k1 = w_in[:, qr + kvr:qr + kvr + half]
    k2 = w_in[:, qr + kvr + half:qr + kvr + MLA_ROPE]
    w_in_p = jnp.concatenate(
        [w_in[:, :qr + kvr], k1, -k2, k2, k1, w_in[:, qr + kvr + MLA_ROPE:]], axis=1).astype(BF16)
    w_lat = jnp.transpose(_fold_query_key(w_uq, w_uk), (1, 0, 2))
    r1 = w_uq[:, :, MLA_NOPE:MLA_NOPE + half]
    r2 = w_uq[:, :, MLA_NOPE + half:]
    wq = jnp.concatenate([w_lat, r1, -r2, r2, r1], axis=2).reshape(qr, MLA_HEADS * HEAD_LANES).T.astype(BF16)
    wuv = jnp.transpose(w_uv, (1, 0, 2)).astype(BF16)
    return w_in_p, wq, wuv


def kernel(x, c, positions, ada_w, ada_b, ln_g, ln_b, e_w_in, gmlp_norm_g, gmlp_norm_b, gmlp_ws,
           gmlp_bs, pool_w, pool_b, pool_scale, e_w_out, o_w_in, mla_q_norm_g, mla_kv_norm_g,
           mla_w_uq, mla_w_uk, mla_w_uv, o_w_out):
    b, s, d = x.shape
    assert s % (L0_STEP_TILES * L0_TILE) == 0 and s % (PROJ_STEP_TILES * Q_TILE) == 0 and Q_TILE % CHUNK == 0
    assert L0_TILE % GMLP_BLOCK == 0 and L0_STEP_TILES % 2 == 0 and ada_w.shape[0] == DEPTH

    mod = _modulation(c, ada_w, ada_b)
    shift, scale, gate = (mod[:, :, i * d:(i + 1) * d].reshape(DEPTH, b, 1, d) for i in range(3))

    x = _layer0(x, shift[0], scale[0], gate[0], e_w_in[0], gmlp_norm_g[0], gmlp_norm_b[0], gmlp_ws[0],
                gmlp_bs[0], pool_w[0], pool_b[0], pool_scale[0], e_w_out[0], ln_g[0], ln_b[0])

    w_in_p, wq, wuv = _layer1_weights(o_w_in[0], mla_w_uq[0], mla_w_uk[0], mla_w_uv[0])
    cos_t, sin_t = _rope_tables(positions)
    q, k, vt, zg = _layer1_proj(x, shift[1], scale[1], cos_t, sin_t, w_in_p, mla_q_norm_g[0], mla_kv_norm_g[0], wq)
    return _layer1_attn(q, k, vt, zg, x, gate[1], wuv, o_w_out[0].astype(BF16), ln_g[1], ln_b[1])
```

```python
import functools
import math

import numpy as np
import jax
import jax.numpy as jnp
from jax import lax
from jax.experimental import pallas as pl
from jax.experimental.pallas import tpu as pltpu

F32 = jnp.float32
BF16 = jnp.bfloat16

CHUNK = 64
LN_EPS = 1e-5
GMLP_HEADS = 4
GMLP_BLOCK = 128
POOL_WINDOWS = (2, 4, 8, 16)
MLA_HEADS = 16
MLA_NOPE = 128
MLA_ROPE = 64
MLA_V = 128
MLA_Q_RANK = 256
MLA_KV_RANK = 128
ROPE_THETA = 10000.0
ATTN_SCALE = (MLA_NOPE + MLA_ROPE) ** -0.5
DEPTH = 2
DEEPNORM_ALPHA = (2.0 * DEPTH) ** 0.25

V7X_LANES = 128
V7X_MXU_WIDTH = 256
V7X_VMEM_BYTES = 64 * 1024 * 1024
VMEM_LIMIT_BYTES = V7X_VMEM_BYTES - 8 * 1024 * 1024

POOL_HALO = max(POOL_WINDOWS)
V7X_SUBLANES = 8
POOL_PAD = V7X_SUBLANES
HEAD_LANES = 2 * V7X_LANES
MASK_VALUE = -1e30
LOG2E = math.log2(math.e)

L0_TILE = 256
L0_STEP_TILES = 4
PROJ_STEP_TILES = 4
FOLD_STEP_HEADS = 4
Q_TILE = 256
ATTN_STEP_TILES = 2
ATTN_COL_GROUPS = MLA_HEADS
EPILOGUE_LAG = 4
PROJ_PIECES = 4
BF16_SUBLANES = 16
VT_ROWS = MLA_KV_RANK + BF16_SUBLANES


def _silu(v):
    return v * jax.nn.sigmoid(v)


def _layer_norm(v, g, b):
    mu = jnp.mean(v, axis=-1, keepdims=True)
    d = v - mu
    var = jnp.mean(d * d, axis=-1, keepdims=True)
    return d * lax.rsqrt(var + LN_EPS) * g + b


def _resident(shape):
    nd = len(shape)
    return pl.BlockSpec(shape, lambda *_: (0,) * nd, pipeline_mode=pl.Buffered(1))


def _mod_kernel(c_ref, w_ref, b_ref, o_ref):
    cond = _silu(c_ref[...]).astype(BF16)
    o_ref[0] = jnp.dot(cond, w_ref[0].astype(BF16), preferred_element_type=F32) + b_ref[0]


def _modulation(c, ada_w, ada_b):
    depth, d, n = ada_w.shape
    b = c.shape[0]
    tn = d
    return pl.pallas_call(
        _mod_kernel,
        grid=(depth, n // tn),
        in_specs=[
            pl.BlockSpec((b, d), lambda l, j: (0, 0)),
            pl.BlockSpec((1, d, tn), lambda l, j: (l, 0, j)),
            pl.BlockSpec((1, 1, tn), lambda l, j: (l, 0, j)),
        ],
        out_specs=pl.BlockSpec((1, b, tn), lambda l, j: (l, 0, j)),
        out_shape=jax.ShapeDtypeStruct((depth, b, n), F32),
        compiler_params=pltpu.CompilerParams(
            dimension_semantics=("arbitrary", "arbitrary"), vmem_limit_bytes=VMEM_LIMIT_BYTES),
        name="adaln_mod",
    )(c, ada_w, ada_b.reshape(depth, 1, n))


def _l0_kernel(xp_ref, xn_ref, shc_ref, scc_ref, shn_ref, scn_ref, gate_ref, w_in_ref, ng_ref, nb_ref,
               ws_ref, bsf_ref, pw_ref, pb_ref, ps_ref, cnt_ref, w_out_ref, lng_ref, lnb_ref,
               o_ref, p0_ref, p1_ref, xbuf_ref, la_ref, lb_ref, mix_ref):
    t, d = xn_ref.shape[1], xn_ref.shape[2]
    width = d
    hd = width // GMLP_HEADS
    gd = width // len(POOL_WINDOWS)
    halo = POOL_PAD + POOL_HALO
    first_step = jnp.logical_and(pl.program_id(0) == 0, pl.program_id(1) == 0)
    seq_start = pl.program_id(1) == 0
    n_pieces = GMLP_HEADS + len(POOL_WINDOWS)
    mxu_cols = w_in_ref.shape[1] // V7X_MXU_WIDTH
    cuts = [V7X_MXU_WIDTH * ((k * mxu_cols) // n_pieces) for k in range(n_pieces + 1)]

    def modulate(x, shift, scale):
        return (x * (1.0 + scale) + shift).astype(BF16)

    def project_piece(h, p_ref, k):
        cols = slice(cuts[k], cuts[k + 1])
        p_ref[:, cols] = jnp.dot(h, w_in_ref[:, cols], preferred_element_type=F32)

    r = lax.broadcasted_iota(jnp.int32, (GMLP_BLOCK, GMLP_BLOCK), 0) // CHUNK
    c = lax.broadcasted_iota(jnp.int32, (GMLP_BLOCK, GMLP_BLOCK), 1) // CHUNK
    tri = c <= r

    def gmlp_head(p_ref, hh):
        cols = slice(hh * hd, (hh + 1) * hd)
        vn = _layer_norm(p_ref[:, width + hh * hd:width + (hh + 1) * hd],
                         ng_ref[...], nb_ref[...]).astype(BF16)
        w = jnp.where(tri, ws_ref[hh], 0.0).astype(BF16)
        for k in range(t // GMLP_BLOCK):
            rows = slice(k * GMLP_BLOCK, (k + 1) * GMLP_BLOCK)
            sv = jnp.dot(w, vn[rows], preferred_element_type=F32) + bsf_ref[:, cols]
            za = p_ref[rows, 2 * width + hh * hd:2 * width + (hh + 1) * hd]
            mix_ref[rows, cols] = (p_ref[rows, cols] * sv * _silu(za)).astype(BF16)

    def pool_fill(p_ref, is_seq_start):
        carry = xbuf_ref[POOL_PAD:halo, :]
        if is_seq_start is not None:
            carry = jnp.where(is_seq_start, 0.0, carry)
            xbuf_ref[POOL_PAD:halo, :] = carry
        xbuf_ref[halo:halo + t, :] = p_ref[:, 3 * width:4 * width]

    def pool_tail(p_ref, g, pooled, rows):
        cols = slice(g * gd, (g + 1) * gd)
        y = jnp.dot(pooled.astype(BF16), pw_ref[g], preferred_element_type=F32)
        zb = p_ref[rows, 4 * width + g * gd:4 * width + (g + 1) * gd]
        mix_ref[rows, width + g * gd:width + (g + 1) * gd] = (
            (y + pb_ref[:, cols]) * ps_ref[:, cols] * _silu(zb)).astype(BF16)

    def pool_group(p_ref, g, is_seq_start):
        win = POOL_WINDOWS[g]
        cols = slice(g * gd, (g + 1) * gd)
        levels = win.bit_length() - 1
        src = xbuf_ref.at[:, cols]
        for lvl in range(1, levels + 1):
            shift = 1 << (lvl - 1)
            lo = halo if lvl == levels else POOL_PAD
            n = halo + t - lo
            val = src[lo:lo + n, :] + src[lo - shift:lo - shift + n, :]
            if lvl < levels:
                dst = la_ref if lvl % 2 else lb_ref
                dst[lo:lo + n, :] = val
                src = dst
        tok = xbuf_ref[halo:halo + t, cols]
        pool_tail(p_ref, g, val * (1.0 / win) - tok, slice(0, t))
        if is_seq_start is not None:
            div = jnp.where(is_seq_start, cnt_ref[:, cols], float(win))
            pool_tail(p_ref, g, val[0:POOL_HALO] / div - tok[0:POOL_HALO], slice(0, POOL_HALO))

    def finish(x, out_rows):
        xbuf_ref[POOL_PAD:halo, :] = xbuf_ref[t + POOL_PAD:t + halo, :]
        y = jnp.dot(mix_ref[...], w_out_ref[...], preferred_element_type=F32)
        res = DEEPNORM_ALPHA * x + (1.0 + gate_ref[0]) * y
        o_ref[0, out_rows, :] = _layer_norm(res, lng_ref[...], lnb_ref[...])

    def half_step(h_next, p_into, p_from, x, out_rows, is_seq_start):
        pool_fill(p_from, is_seq_start)
        for k in range(n_pieces):
            project_piece(h_next, p_into, k)
            if k < GMLP_HEADS:
                gmlp_head(p_from, k)
            else:
                pool_group(p_from, k - GMLP_HEADS, is_seq_start)
        finish(x, out_rows)

    @pl.when(first_step)
    def _():
        for buf in (xbuf_ref, la_ref, lb_ref):
            buf[0:POOL_PAD, :] = jnp.zeros((POOL_PAD, buf.shape[1]), F32)
        h = modulate(xp_ref[0, 0:t, :], shc_ref[0], scc_ref[0])
        for k in range(n_pieces):
            project_piece(h, p0_ref, k)

    slots = (p0_ref, p1_ref)
    for i in range(L0_STEP_TILES):
        rows = slice(i * t, (i + 1) * t)
        if i + 1 < L0_STEP_TILES:
            h_next = modulate(xp_ref[0, (i + 1) * t:(i + 2) * t, :], shc_ref[0], scc_ref[0])
        else:
            h_next = modulate(xn_ref[0], shn_ref[0], scn_ref[0])
        half_step(h_next, slots[(i + 1) % 2], slots[i % 2], xp_ref[0, rows, :], rows,
                  seq_start if i == 0 else None)


def _layer0(x, shift, scale, gate, w_in, ng, nb, ws, bs, pw, pb, ps, w_out, lng, lnb):
    b, s, d = x.shape
    t = L0_TILE
    width = d
    hd = width // GMLP_HEADS
    gd = width // len(POOL_WINDOWS)
    bsf = jnp.broadcast_to(bs.T[:, :, None], (GMLP_BLOCK, GMLP_HEADS, hd)).reshape(GMLP_BLOCK, width)
    pos = np.arange(POOL_HALO, dtype=np.float32)[:, None] + 1.0
    cnt = np.concatenate(
        [np.broadcast_to(np.minimum(pos, float(w)), (POOL_HALO, gd)) for w in POOL_WINDOWS], axis=1)
    row = lambda a: a.reshape(1, -1)
    tiles = s // t
    per_step = L0_STEP_TILES

    def next_tile(i, j):
        flat = jnp.minimum(i * tiles + per_step * (j + 1), b * tiles - 1)
        return flat // tiles, flat % tiles

    vec = pl.BlockSpec((1, 1, d), lambda i, j: (i, 0, 0))
    vec_next = pl.BlockSpec((1, 1, d), lambda i, j: (next_tile(i, j)[0], 0, 0))
    return pl.pallas_call(
        _l0_kernel,
        grid=(b, tiles // per_step),
        in_specs=[
            pl.BlockSpec((1, per_step * t, d), lambda i, j: (i, j, 0)),
            pl.BlockSpec((1, t, d), lambda i, j: (*next_tile(i, j), 0)),
            vec, vec, vec_next, vec_next, vec,
            _resident(w_in.shape),
            _resident((1, hd)), _resident((1, hd)),
            _resident(ws.shape),
            _resident(bsf.shape),
            _resident(pw.shape),
            _resident((1, width)), _resident((1, width)),
            _resident(cnt.shape),
            _resident(w_out.shape),
            _resident((1, d)), _resident((1, d)),
        ],
        out_specs=pl.BlockSpec((1, per_step * t, d), lambda i, j: (i, j, 0)),
        out_shape=jax.ShapeDtypeStruct((b, s, d), F32),
        scratch_shapes=[
            pltpu.VMEM((t, w_in.shape[1]), F32),
            pltpu.VMEM((t, w_in.shape[1]), F32),
            pltpu.VMEM((POOL_PAD + POOL_HALO + t, width), F32),
            pltpu.VMEM((POOL_PAD + POOL_HALO + t, gd), F32),
            pltpu.VMEM((POOL_PAD + POOL_HALO + t, gd), F32),
            pltpu.VMEM((t, 2 * width), BF16),
        ],
        compiler_params=pltpu.CompilerParams(
            dimension_semantics=("arbitrary", "arbitrary"), vmem_limit_bytes=VMEM_LIMIT_BYTES),
        name="layer0_gmlp_pool",
    )(x, x, shift, scale, shift, scale, gate, w_in.astype(BF16), row(ng), row(nb), ws, bsf,
      pw.astype(BF16), row(pb), row(ps), jnp.asarray(cnt), w_out.astype(BF16), row(lng), row(lnb))


def _fold_kernel(wq_ref, wk_ref, o_ref):
    for hh in range(wq_ref.shape[0]):
        o_ref[hh] = lax.dot_general(wq_ref[hh], wk_ref[hh], (((1,), (1,)), ((), ())),
                                    precision=lax.Precision.HIGHEST, preferred_element_type=F32)


def _fold_query_key(w_uq, w_uk):
    qr = w_uq.shape[0]
    wq = jnp.transpose(w_uq[:, :, :MLA_NOPE], (1, 0, 2))
    wk = jnp.transpose(w_uk, (1, 0, 2))
    return pl.pallas_call(
        _fold_kernel,
        grid=(MLA_HEADS // FOLD_STEP_HEADS,),
        in_specs=[pl.BlockSpec((FOLD_STEP_HEADS, qr, MLA_NOPE), lambda h: (h, 0, 0)),
                  pl.BlockSpec((FOLD_STEP_HEADS, MLA_KV_RANK, MLA_NOPE), lambda h: (h, 0, 0))],
        out_specs=pl.BlockSpec((FOLD_STEP_HEADS, qr, MLA_KV_RANK), lambda h: (h, 0, 0)),
        out_shape=jax.ShapeDtypeStruct((MLA_HEADS, qr, MLA_KV_RANK), F32),
        name="fold_uq_uk",
    )(wq, wk)


def _rope_kernel(pos_ref, inv_ref, cos_ref, sin_ref):
    ang = inv_ref[...] * pos_ref[0].astype(F32)
    cos_ref[0] = jnp.cos(ang)
    sin_ref[0] = jnp.sin(ang)


def _rope_tables(positions):
    b, s = positions.shape
    half = MLA_ROPE // 2
    inv = (1.0 / (ROPE_THETA ** (np.arange(0, MLA_ROPE, 2, dtype=np.float64) / MLA_ROPE))).astype(np.float32)
    table = pl.BlockSpec((1, half, s), lambda i: (i, 0, 0))
    return pl.pallas_call(
        _rope_kernel,
        grid=(b,),
        in_specs=[pl.BlockSpec((1, 1, s), lambda i: (i, 0, 0)), _resident((half, 1))],
        out_specs=[table, table],
        out_shape=[jax.ShapeDtypeStruct((b, half, s), F32)] * 2,
        name="rope_tables",
    )(positions.reshape(b, 1, s), jnp.asarray(inv[:, None]))


def _l1_proj_kernel(x_ref, shift_ref, scale_ref, cos_ref, sin_ref, w_in_ref, qg_ref, kvg_ref, wq_ref,
                    q_ref, k_ref, vt_ref, zg_ref):
    t = x_ref.shape[1] // PROJ_STEP_TILES
    qr, kvr = MLA_Q_RANK, MLA_KV_RANK
    for i in range(PROJ_STEP_TILES):
        rows = slice(i * t, (i + 1) * t)
        _project_tile(x_ref[0, rows, :], shift_ref[0], scale_ref[0], cos_ref[0, :, rows], sin_ref[0, :, rows],
                      w_in_ref, qg_ref, kvg_ref, wq_ref, q_ref.at[0, i], k_ref.at[0, rows, :],
                      vt_ref.at[0, :, rows], zg_ref.at[0, rows, :], t, qr, kvr)


def _project_tile(x, shift, scale, cos_t, sin_t, w_in_ref, qg_ref, kvg_ref, wq_ref,
                  q_ref, k_ref, vt_ref, zg_ref, t, qr, kvr):
    h = (x * (1.0 + scale) + shift).astype(BF16)
    cs_t = jnp.concatenate([cos_t, sin_t, cos_t, sin_t], axis=0)
    cs = cs_t.T

    lat = jnp.dot(h, w_in_ref[:, 0:qr + kvr + V7X_LANES], preferred_element_type=F32)
    q_c = lat[:, 0:qr]
    kv_c = lat[:, qr:qr + kvr]
    k_r = lat[:, qr + kvr:qr + kvr + V7X_LANES]

    q_c = q_c * lax.rsqrt(jnp.mean(q_c * q_c, axis=-1, keepdims=True) + LN_EPS) * qg_ref[...]
    kv_c = kv_c * lax.rsqrt(jnp.mean(kv_c * kv_c, axis=-1, keepdims=True) + LN_EPS) * kvg_ref[...]

    kt = k_r * cs
    quarter = V7X_LANES // 4
    lane = lax.broadcasted_iota(jnp.int32, kt.shape, 1)
    swapped = jnp.where((lane // quarter) % 2 == 0,
                        pltpu.roll(kt, V7X_LANES - quarter, 1), pltpu.roll(kt, quarter, 1))
    k_ref[...] = jnp.concatenate([kv_c, kt + swapped], axis=-1).astype(BF16)
    vt_ref[0:kvr, :] = kv_c.T.astype(BF16)
    vt_ref[kvr:VT_ROWS, :] = jnp.ones((VT_ROWS - kvr, t), BF16)

    qn_t = (q_c * (ATTN_SCALE * LOG2E)).T.astype(BF16)

    def query_heads(g):
        per = MLA_HEADS // PROJ_PIECES
        qf_t = jnp.dot(wq_ref[g * per * HEAD_LANES:(g + 1) * per * HEAD_LANES, :], qn_t,
                       preferred_element_type=F32)
        for k in range(per):
            hh = g * per + k
            q_lat = qf_t[k * HEAD_LANES:k * HEAD_LANES + V7X_LANES]
            q_rope = qf_t[k * HEAD_LANES + V7X_LANES:(k + 1) * HEAD_LANES] * cs_t
            q_ref[hh] = jnp.concatenate([q_lat, q_rope], axis=0).astype(BF16)

    def gate_cols(g):
        zw = zg_ref.shape[1] // PROJ_PIECES
        z0 = qr + kvr + V7X_LANES + g * zw
        z = jnp.dot(h, w_in_ref[:, z0:z0 + zw], preferred_element_type=F32)
        zg_ref[:, g * zw:(g + 1) * zw] = _silu(z).astype(BF16)

    for g in range(PROJ_PIECES):
        gate_cols(g)
        query_heads(g)


def _layer1_proj(x, shift, scale, cos_t, sin_t, w_in, qg, kvg, wq):
    b, s, d = x.shape
    t = Q_TILE
    n = PROJ_STEP_TILES
    zw = w_in.shape[1] - (MLA_Q_RANK + MLA_KV_RANK + V7X_LANES)
    vec = pl.BlockSpec((1, 1, d), lambda i, j: (i, 0, 0))
    return pl.pallas_call(
        _l1_proj_kernel,
        grid=(b, s // (n * t)),
        in_specs=[
            pl.BlockSpec((1, n * t, d), lambda i, j: (i, j, 0)),
            vec, vec,
            pl.BlockSpec((1, MLA_ROPE // 2, n * t), lambda i, j: (i, 0, j)),
            pl.BlockSpec((1, MLA_ROPE // 2, n * t), lambda i, j: (i, 0, j)),
            _resident(w_in.shape),
            _resident((1, MLA_Q_RANK)), _resident((1, MLA_KV_RANK)),
            _resident(wq.shape),
        ],
        out_specs=[
            pl.BlockSpec((1, n, MLA_HEADS, HEAD_LANES, t), lambda i, j: (i, j, 0, 0, 0)),
            pl.BlockSpec((1, n * t, HEAD_LANES), lambda i, j: (i, j, 0)),
            pl.BlockSpec((1, VT_ROWS, n * t), lambda i, j: (i, 0, j)),
            pl.BlockSpec((1, n * t, zw), lambda i, j: (i, j, 0)),
        ],
        out_shape=[
            jax.ShapeDtypeStruct((b, s // t, MLA_HEADS, HEAD_LANES, t), BF16),
            jax.ShapeDtypeStruct((b, s, HEAD_LANES), BF16),
            jax.ShapeDtypeStruct((b, VT_ROWS, s), BF16),
            jax.ShapeDtypeStruct((b, s, zw), BF16),
        ],
        compiler_params=pltpu.CompilerParams(
            dimension_semantics=("arbitrary", "arbitrary"), vmem_limit_bytes=VMEM_LIMIT_BYTES),
        name="layer1_proj",
    )(x, shift, scale, cos_t, sin_t, w_in, qg.reshape(1, -1), kvg.reshape(1, -1), wq)


def _attn_kernel(q_ref, k_ref, vt_ref, zg_ref, x_ref, gate_ref, wuv_ref, w_out_ref, lng_ref, lnb_ref,
                 o_ref, s0_ref, s1_ref, mc0_ref, mc1_ref, m_ref, acc_ref, mix_ref):
    qt = x_ref.shape[1] // ATTN_STEP_TILES
    slot0, slot1 = (s0_ref, mc0_ref), (s1_ref, mc1_ref)
    assert ATTN_COL_GROUPS == MLA_HEADS and ATTN_STEP_TILES == 2

    for local in range(ATTN_STEP_TILES):
        qi = ATTN_STEP_TILES * pl.program_id(1) + local
        rows = slice(local * qt, (local + 1) * qt)

        def scores(start, s_ref, mc_ref, masked, g, local=local):
            start = pl.multiple_of(start, qt)
            st = jnp.dot(k_ref[0, pl.ds(start, qt), :], q_ref[0, local, g], preferred_element_type=F32)
            if masked:
                k_chunk = lax.broadcasted_iota(jnp.int32, (qt, qt), 0) // CHUNK
                q_chunk = lax.broadcasted_iota(jnp.int32, (qt, qt), 1) // CHUNK
                st = jnp.where(k_chunk <= q_chunk, st, MASK_VALUE)
            s_ref[g] = st
            mc_ref[g] = jnp.max(st, axis=0, keepdims=True)

        def accumulate(start, s_ref, mc_ref, g):
            start = pl.multiple_of(start, qt)
            m_prev = m_ref[g]
            m_new = jnp.maximum(m_prev, mc_ref[g])
            alpha = jnp.exp2(m_prev - m_new)
            pt = jnp.exp2(s_ref[g] - m_new).astype(BF16)
            pv = jnp.dot(vt_ref[0, :, pl.ds(start, qt)], pt, preferred_element_type=F32)
            acc_ref[g] = alpha * acc_ref[g] + pv
            m_ref[g] = m_new

        m_ref[...] = jnp.full(m_ref.shape, MASK_VALUE, F32)
        acc_ref[...] = jnp.zeros(acc_ref.shape, F32)

        def folded_start(i, qi=qi):
            return jnp.where(i == 0, qi, i - 1) * qt

        def trip(i, into, fold, scores=scores, accumulate=accumulate, folded_start=folded_start):
            for g in range(ATTN_COL_GROUPS):
                scores(i * qt, *into, False, g)
                accumulate(folded_start(i), *fold, g)

        odd = local % 2
        for g in range(ATTN_COL_GROUPS):
            scores(qi * qt, *(slot1 if odd else slot0), True, g)
        if odd:
            trip(0, slot0, slot1)

        def pair(p, carry, trip=trip, odd=odd):
            trip(2 * p + odd, slot1, slot0)
            trip(2 * p + odd + 1, slot0, slot1)
            return carry

        lax.fori_loop(0, pl.program_id(1), pair, 0)

        def gated_value(hh, rows=rows):
            inv_l = 1.0 / acc_ref[hh, MLA_KV_RANK:MLA_KV_RANK + 1, :]
            o_t = (acc_ref[hh, 0:MLA_KV_RANK, :] * inv_l).astype(BF16)
            o_h = lax.dot_general(o_t, wuv_ref[hh], (((0,), (0,)), ((), ())),
                                  preferred_element_type=F32)
            cols = slice(hh * MLA_V, (hh + 1) * MLA_V)
            mix_ref[:, cols] = (o_h * zg_ref[0, rows, cols].astype(F32)).astype(BF16)

        for hh in range(MLA_HEADS + EPILOGUE_LAG):
            if hh < MLA_HEADS:
                accumulate(folded_start(qi), *slot0, hh)
            if hh >= EPILOGUE_LAG:
                gated_value(hh - EPILOGUE_LAG)
        y = jnp.dot(mix_ref[...], w_out_ref[...], preferred_element_type=F32)
        res = DEEPNORM_ALPHA * x_ref[0, rows, :] + (1.0 + gate_ref[0]) * y
        o_ref[0, rows, :] = _layer_norm(res, lng_ref[...], lnb_ref[...])


def _layer1_attn(q, k, vt, zg, x, gate, wuv, w_out, lng, lnb):
    b, s, d = x.shape
    qt = Q_TILE
    n = ATTN_STEP_TILES
    width = zg.shape[2]
    return pl.pallas_call(
        _attn_kernel,
        grid=(b, s // (n * qt)),
        in_specs=[
            pl.BlockSpec((1, n, MLA_HEADS, HEAD_LANES, qt), lambda i, j: (i, j, 0, 0, 0)),
            pl.BlockSpec((1, s, HEAD_LANES), lambda i, j: (i, 0, 0)),
            pl.BlockSpec((1, VT_ROWS, s), lambda i, j: (i, 0, 0)),
            pl.BlockSpec((1, n * qt, width), lambda i, j: (i, j, 0)),
            pl.BlockSpec((1, n * qt, d), lambda i, j: (i, j, 0)),
            pl.BlockSpec((1, 1, d), lambda i, j: (i, 0, 0)),
            _resident(wuv.shape),
            _resident(w_out.shape),
            _resident((1, d)), _resident((1, d)),
        ],
        out_specs=pl.BlockSpec((1, n * qt, d), lambda i, j: (i, j, 0)),
        out_shape=jax.ShapeDtypeStruct((b, s, d), F32),
        scratch_shapes=[
            pltpu.VMEM((MLA_HEADS, qt, qt), F32),
            pltpu.VMEM((MLA_HEADS, qt, qt), F32),
            pltpu.VMEM((MLA_HEADS, 1, qt), F32),
            pltpu.VMEM((MLA_HEADS, 1, qt), F32),
            pltpu.VMEM((MLA_HEADS, 1, qt), F32),
            pltpu.VMEM((MLA_HEADS, VT_ROWS, qt), F32),
            pltpu.VMEM((qt, width), BF16),
        ],
        compiler_params=pltpu.CompilerParams(
            dimension_semantics=("arbitrary", "arbitrary"), vmem_limit_bytes=VMEM_LIMIT_BYTES),
        name="layer1_attn",
    )(q, k, vt, zg, x, gate, wuv, w_out, lng.reshape(1, -1), lnb.reshape(1, -1))


def _layer1_weights(w_in, w_uq, w_uk, w_uv):
    qr, kvr, half = MLA_Q_RANK, MLA_KV_RANK, MLA_ROPE // 2
    k1 = w_in[:, qr + kvr:qr + kvr + half]
    k2 = w_in[:, qr + kvr + half:qr + kvr + MLA_ROPE]
    w_in_p = jnp.concatenate(
        [w_in[:, :qr + kvr], k1, -k2, k2, k1, w_in[:, qr + kvr + MLA_ROPE:]], axis=1).astype(BF16)
    w_lat = jnp.transpose(_fold_query_key(w_uq, w_uk), (1, 0, 2))
    r1 = w_uq[:, :, MLA_NOPE:MLA_NOPE + half]
    r2 = w_uq[:, :, MLA_NOPE + half:]
    wq = jnp.concatenate([w_lat, r1, -r2, r2, r1], axis=2).reshape(qr, MLA_HEADS * HEAD_LANES).T.astype(BF16)
    wuv = jnp.transpose(w_uv, (1, 0, 2)).astype(BF16)
    return w_in_p, wq, wuv


def kernel(x, c, positions, ada_w, ada_b, ln_g, ln_b, e_w_in, gmlp_norm_g, gmlp_norm_b, gmlp_ws,
           gmlp_bs, pool_w, pool_b, pool_scale, e_w_out, o_w_in, mla_q_norm_g, mla_kv_norm_g,
           mla_w_uq, mla_w_uk, mla_w_uv, o_w_out):
    b, s, d = x.shape
    assert s % (L0_STEP_TILES * L0_TILE) == 0 and s % (PROJ_STEP_TILES * Q_TILE) == 0 and Q_TILE % CHUNK == 0
    assert s % (ATTN_STEP_TILES * Q_TILE) == 0
    assert L0_TILE % GMLP_BLOCK == 0 and L0_STEP_TILES % 2 == 0 and ada_w.shape[0] == DEPTH

    mod = _modulation(c, ada_w, ada_b)
    shift, scale, gate = (mod[:, :, i * d:(i + 1) * d].reshape(DEPTH, b, 1, d) for i in range(3))

    x = _layer0(x, shift[0], scale[0], gate[0], e_w_in[0], gmlp_norm_g[0], gmlp_norm_b[0], gmlp_ws[0],
                gmlp_bs[0], pool_w[0], pool_b[0], pool_scale[0], e_w_out[0], ln_g[0], ln_b[0])

    w_in_p, wq, wuv = _layer1_weights(o_w_in[0], mla_w_uq[0], mla_w_uk[0], mla_w_uv[0])
    cos_t, sin_t = _rope_tables(positions)
    q, k, vt, zg = _layer1_proj(x, shift[1], scale[1], cos_t, sin_t, w_in_p, mla_q_norm_g[0], mla_kv_norm_g[0], wq)
    return _layer1_attn(q, k, vt, zg, x, gate[1], wuv, o_w_out[0].astype(BF16), ln_g[1], ln_b[1])
```

```python
import functools
import math

import numpy as np
import jax
import jax.numpy as jnp
from jax import lax
from jax.experimental import pallas as pl
from jax.experimental.pallas import tpu as pltpu

F32 = jnp.float32
BF16 = jnp.bfloat16

CHUNK = 64
LN_EPS = 1e-5
GMLP_HEADS = 4
GMLP_BLOCK = 128
POOL_WINDOWS = (2, 4, 8, 16)
MLA_HEADS = 16
MLA_NOPE = 128
MLA_ROPE = 64
MLA_V = 128
MLA_Q_RANK = 256
MLA_KV_RANK = 128
ROPE_THETA = 10000.0
ATTN_SCALE = (MLA_NOPE + MLA_ROPE) ** -0.5
DEPTH = 2
DEEPNORM_ALPHA = (2.0 * DEPTH) ** 0.25

V7X_LANES = 128
V7X_MXU_WIDTH = 256
V7X_VMEM_BYTES = 64 * 1024 * 1024
VMEM_LIMIT_BYTES = V7X_VMEM_BYTES - 8 * 1024 * 1024

POOL_HALO = max(POOL_WINDOWS)
V7X_SUBLANES = 8
POOL_PAD = V7X_SUBLANES
HEAD_LANES = 2 * V7X_LANES
MASK_VALUE = -1e30
LOG2E = math.log2(math.e)

L0_TILE = 256
L0_STEP_TILES = 4
PROJ_STEP_TILES = 4
FOLD_STEP_HEADS = 4
Q_TILE = 256
ATTN_STEP_TILES = 2
ATTN_COL_GROUPS = MLA_HEADS
EPILOGUE_LAG = MLA_HEADS
PROJ_PIECES = 4
BF16_SUBLANES = 16
VT_ROWS = MLA_KV_RANK + BF16_SUBLANES


def _silu(v):
    return v * jax.nn.sigmoid(v)


def _layer_norm(v, g, b):
    mu = jnp.mean(v, axis=-1, keepdims=True)
    d = v - mu
    var = jnp.mean(d * d, axis=-1, keepdims=True)
    return d * lax.rsqrt(var + LN_EPS) * g + b


def _resident(shape):
    nd = len(shape)
    return pl.BlockSpec(shape, lambda *_: (0,) * nd, pipeline_mode=pl.Buffered(1))


def _mod_kernel(c_ref, w_ref, b_ref, o_ref):
    cond = _silu(c_ref[...]).astype(BF16)
    o_ref[0] = jnp.dot(cond, w_ref[0].astype(BF16), preferred_element_type=F32) + b_ref[0]


def _modulation(c, ada_w, ada_b):
    depth, d, n = ada_w.shape
    b = c.shape[0]
    tn = d
    return pl.pallas_call(
        _mod_kernel,
        grid=(depth, n // tn),
        in_specs=[
            pl.BlockSpec((b, d), lambda l, j: (0, 0)),
            pl.BlockSpec((1, d, tn), lambda l, j: (l, 0, j)),
            pl.BlockSpec((1, 1, tn), lambda l, j: (l, 0, j)),
        ],
        out_specs=pl.BlockSpec((1, b, tn), lambda l, j: (l, 0, j)),
        out_shape=jax.ShapeDtypeStruct((depth, b, n), F32),
        compiler_params=pltpu.CompilerParams(
            dimension_semantics=("arbitrary", "arbitrary"), vmem_limit_bytes=VMEM_LIMIT_BYTES),
        name="adaln_mod",
    )(c, ada_w, ada_b.reshape(depth, 1, n))


def _l0_kernel(xp_ref, xn_ref, shc_ref, scc_ref, shn_ref, scn_ref, gate_ref, w_in_ref, ng_ref, nb_ref,
               ws_ref, bsf_ref, pw_ref, pb_ref, ps_ref, cnt_ref, w_out_ref, lng_ref, lnb_ref,
               o_ref, p0_ref, p1_ref, xbuf_ref, la_ref, lb_ref, mix_ref):
    t, d = xn_ref.shape[1], xn_ref.shape[2]
    width = d
    hd = width // GMLP_HEADS
    gd = width // len(POOL_WINDOWS)
    halo = POOL_PAD + POOL_HALO
    first_step = jnp.logical_and(pl.program_id(0) == 0, pl.program_id(1) == 0)
    seq_start = pl.program_id(1) == 0
    n_pieces = GMLP_HEADS + len(POOL_WINDOWS)
    mxu_cols = w_in_ref.shape[1] // V7X_MXU_WIDTH
    cuts = [V7X_MXU_WIDTH * ((k * mxu_cols) // n_pieces) for k in range(n_pieces + 1)]

    def modulate(x, shift, scale):
        return (x * (1.0 + scale) + shift).astype(BF16)

    def project_piece(h, p_ref, k):
        cols = slice(cuts[k], cuts[k + 1])
        p_ref[:, cols] = jnp.dot(h, w_in_ref[:, cols], preferred_element_type=F32)

    r = lax.broadcasted_iota(jnp.int32, (GMLP_BLOCK, GMLP_BLOCK), 0) // CHUNK
    c = lax.broadcasted_iota(jnp.int32, (GMLP_BLOCK, GMLP_BLOCK), 1) // CHUNK
    tri = c <= r

    def gmlp_head(p_ref, hh):
        cols = slice(hh * hd, (hh + 1) * hd)
        vn = _layer_norm(p_ref[:, width + hh * hd:width + (hh + 1) * hd],
                         ng_ref[...], nb_ref[...]).astype(BF16)
        w = jnp.where(tri, ws_ref[hh], 0.0).astype(BF16)
        for k in range(t // GMLP_BLOCK):
            rows = slice(k * GMLP_BLOCK, (k + 1) * GMLP_BLOCK)
            sv = jnp.dot(w, vn[rows], preferred_element_type=F32) + bsf_ref[:, cols]
            za = p_ref[rows, 2 * width + hh * hd:2 * width + (hh + 1) * hd]
            mix_ref[rows, cols] = (p_ref[rows, cols] * sv * _silu(za)).astype(BF16)

    def pool_fill(p_ref, is_seq_start):
        carry = xbuf_ref[POOL_PAD:halo, :]
        if is_seq_start is not None:
            carry = jnp.where(is_seq_start, 0.0, carry)
            xbuf_ref[POOL_PAD:halo, :] = carry
        xbuf_ref[halo:halo + t, :] = p_ref[:, 3 * width:4 * width]

    def pool_tail(p_ref, g, pooled, rows):
        cols = slice(g * gd, (g + 1) * gd)
        y = jnp.dot(pooled.astype(BF16), pw_ref[g], preferred_element_type=F32)
        zb = p_ref[rows, 4 * width + g * gd:4 * width + (g + 1) * gd]
        mix_ref[rows, width + g * gd:width + (g + 1) * gd] = (
            (y + pb_ref[:, cols]) * ps_ref[:, cols] * _silu(zb)).astype(BF16)

    def pool_group(p_ref, g, is_seq_start):
        win = POOL_WINDOWS[g]
        cols = slice(g * gd, (g + 1) * gd)
        levels = win.bit_length() - 1
        src = xbuf_ref.at[:, cols]
        for lvl in range(1, levels + 1):
            shift = 1 << (lvl - 1)
            lo = halo if lvl == levels else POOL_PAD
            n = halo + t - lo
            val = src[lo:lo + n, :] + src[lo - shift:lo - shift + n, :]
            if lvl < levels:
                dst = la_ref if lvl % 2 else lb_ref
                dst[lo:lo + n, :] = val
                src = dst
        tok = xbuf_ref[halo:halo + t, cols]
        pool_tail(p_ref, g, val * (1.0 / win) - tok, slice(0, t))
        if is_seq_start is not None:
            div = jnp.where(is_seq_start, cnt_ref[:, cols], float(win))
            pool_tail(p_ref, g, val[0:POOL_HALO] / div - tok[0:POOL_HALO], slice(0, POOL_HALO))

    def finish(x, out_rows):
        xbuf_ref[POOL_PAD:halo, :] = xbuf_ref[t + POOL_PAD:t + halo, :]
        y = jnp.dot(mix_ref[...], w_out_ref[...], preferred_element_type=F32)
        res = DEEPNORM_ALPHA * x + (1.0 + gate_ref[0]) * y
        o_ref[0, out_rows, :] = _layer_norm(res, lng_ref[...], lnb_ref[...])

    def half_step(h_next, p_into, p_from, x, out_rows, is_seq_start):
        pool_fill(p_from, is_seq_start)
        for k in range(n_pieces):
            project_piece(h_next, p_into, k)
            if k < GMLP_HEADS:
                gmlp_head(p_from, k)
            else:
                pool_group(p_from, k - GMLP_HEADS, is_seq_start)
        finish(x, out_rows)

    @pl.when(first_step)
    def _():
        for buf in (xbuf_ref, la_ref, lb_ref):
            buf[0:POOL_PAD, :] = jnp.zeros((POOL_PAD, buf.shape[1]), F32)
        h = modulate(xp_ref[0, 0:t, :], shc_ref[0], scc_ref[0])
        for k in range(n_pieces):
            project_piece(h, p0_ref, k)

    slots = (p0_ref, p1_ref)
    for i in range(L0_STEP_TILES):
        rows = slice(i * t, (i + 1) * t)
        if i + 1 < L0_STEP_TILES:
            h_next = modulate(xp_ref[0, (i + 1) * t:(i + 2) * t, :], shc_ref[0], scc_ref[0])
        else:
            h_next = modulate(xn_ref[0], shn_ref[0], scn_ref[0])
        half_step(h_next, slots[(i + 1) % 2], slots[i % 2], xp_ref[0, rows, :], rows,
                  seq_start if i == 0 else None)


def _layer0(x, shift, scale, gate, w_in, ng, nb, ws, bs, pw, pb, ps, w_out, lng, lnb):
    b, s, d = x.shape
    t = L0_TILE
    width = d
    hd = width // GMLP_HEADS
    gd = width // len(POOL_WINDOWS)
    bsf = jnp.broadcast_to(bs.T[:, :, None], (GMLP_BLOCK, GMLP_HEADS, hd)).reshape(GMLP_BLOCK, width)
    pos = np.arange(POOL_HALO, dtype=np.float32)[:, None] + 1.0
    cnt = np.concatenate(
        [np.broadcast_to(np.minimum(pos, float(w)), (POOL_HALO, gd)) for w in POOL_WINDOWS], axis=1)
    row = lambda a: a.reshape(1, -1)
    tiles = s // t
    per_step = L0_STEP_TILES

    def next_tile(i, j):
        flat = jnp.minimum(i * tiles + per_step * (j + 1), b * tiles - 1)
        return flat // tiles, flat % tiles

    vec = pl.BlockSpec((1, 1, d), lambda i, j: (i, 0, 0))
    vec_next = pl.BlockSpec((1, 1, d), lambda i, j: (next_tile(i, j)[0], 0, 0))
    return pl.pallas_call(
        _l0_kernel,
        grid=(b, tiles // per_step),
        in_specs=[
            pl.BlockSpec((1, per_step * t, d), lambda i, j: (i, j, 0)),
            pl.BlockSpec((1, t, d), lambda i, j: (*next_tile(i, j), 0)),
            vec, vec, vec_next, vec_next, vec,
            _resident(w_in.shape),
            _resident((1, hd)), _resident((1, hd)),
            _resident(ws.shape),
            _resident(bsf.shape),
            _resident(pw.shape),
            _resident((1, width)), _resident((1, width)),
            _resident(cnt.shape),
            _resident(w_out.shape),
            _resident((1, d)), _resident((1, d)),
        ],
        out_specs=pl.BlockSpec((1, per_step * t, d), lambda i, j: (i, j, 0)),
        out_shape=jax.ShapeDtypeStruct((b, s, d), F32),
        scratch_shapes=[
            pltpu.VMEM((t, w_in.shape[1]), F32),
            pltpu.VMEM((t, w_in.shape[1]), F32),
            pltpu.VMEM((POOL_PAD + POOL_HALO + t, width), F32),
            pltpu.VMEM((POOL_PAD + POOL_HALO + t, gd), F32),
            pltpu.VMEM((POOL_PAD + POOL_HALO + t, gd), F32),
            pltpu.VMEM((t, 2 * width), BF16),
        ],
        compiler_params=pltpu.CompilerParams(
            dimension_semantics=("arbitrary", "arbitrary"), vmem_limit_bytes=VMEM_LIMIT_BYTES),
        name="layer0_gmlp_pool",
    )(x, x, shift, scale, shift, scale, gate, w_in.astype(BF16), row(ng), row(nb), ws, bsf,
      pw.astype(BF16), row(pb), row(ps), jnp.asarray(cnt), w_out.astype(BF16), row(lng), row(lnb))


def _fold_kernel(wq_ref, wk_ref, o_ref):
    for hh in range(wq_ref.shape[0]):
        o_ref[hh] = lax.dot_general(wq_ref[hh], wk_ref[hh], (((1,), (1,)), ((), ())),
                                    precision=lax.Precision.HIGHEST, preferred_element_type=F32)


def _fold_query_key(w_uq, w_uk):
    qr = w_uq.shape[0]
    wq = jnp.transpose(w_uq[:, :, :MLA_NOPE], (1, 0, 2))
    wk = jnp.transpose(w_uk, (1, 0, 2))
    return pl.pallas_call(
        _fold_kernel,
        grid=(MLA_HEADS // FOLD_STEP_HEADS,),
        in_specs=[pl.BlockSpec((FOLD_STEP_HEADS, qr, MLA_NOPE), lambda h: (h, 0, 0)),
                  pl.BlockSpec((FOLD_STEP_HEADS, MLA_KV_RANK, MLA_NOPE), lambda h: (h, 0, 0))],
        out_specs=pl.BlockSpec((FOLD_STEP_HEADS, qr, MLA_KV_RANK), lambda h: (h, 0, 0)),
        out_shape=jax.ShapeDtypeStruct((MLA_HEADS, qr, MLA_KV_RANK), F32),
        name="fold_uq_uk",
    )(wq, wk)


def _rope_kernel(pos_ref, inv_ref, cos_ref, sin_ref):
    ang = inv_ref[...] * pos_ref[0].astype(F32)
    cos_ref[0] = jnp.cos(ang)
    sin_ref[0] = jnp.sin(ang)


def _rope_tables(positions):
    b, s = positions.shape
    half = MLA_ROPE // 2
    inv = (1.0 / (ROPE_THETA ** (np.arange(0, MLA_ROPE, 2, dtype=np.float64) / MLA_ROPE))).astype(np.float32)
    table = pl.BlockSpec((1, half, s), lambda i: (i, 0, 0))
    return pl.pallas_call(
        _rope_kernel,
        grid=(b,),
        in_specs=[pl.BlockSpec((1, 1, s), lambda i: (i, 0, 0)), _resident((half, 1))],
        out_specs=[table, table],
        out_shape=[jax.ShapeDtypeStruct((b, half, s), F32)] * 2,
        name="rope_tables",
    )(positions.reshape(b, 1, s), jnp.asarray(inv[:, None]))


def _l1_proj_kernel(x_ref, shift_ref, scale_ref, cos_ref, sin_ref, w_in_ref, qg_ref, kvg_ref, wq_ref,
                    q_ref, k_ref, vt_ref, zg_ref):
    t = x_ref.shape[1] // PROJ_STEP_TILES
    qr, kvr = MLA_Q_RANK, MLA_KV_RANK
    for i in range(PROJ_STEP_TILES):
        rows = slice(i * t, (i + 1) * t)
        _project_tile(x_ref[0, rows, :], shift_ref[0], scale_ref[0], cos_ref[0, :, rows], sin_ref[0, :, rows],
                      w_in_ref, qg_ref, kvg_ref, wq_ref, q_ref.at[0, i], k_ref.at[0, rows, :],
                      vt_ref.at[0, :, rows], zg_ref.at[0, rows, :], t, qr, kvr)


def _project_tile(x, shift, scale, cos_t, sin_t, w_in_ref, qg_ref, kvg_ref, wq_ref,
                  q_ref, k_ref, vt_ref, zg_ref, t, qr, kvr):
    h = (x * (1.0 + scale) + shift).astype(BF16)
    cs_t = jnp.concatenate([cos_t, sin_t, cos_t, sin_t], axis=0)
    cs = cs_t.T

    lat = jnp.dot(h, w_in_ref[:, 0:qr + kvr + V7X_LANES], preferred_element_type=F32)
    q_c = lat[:, 0:qr]
    kv_c = lat[:, qr:qr + kvr]
    k_r = lat[:, qr + kvr:qr + kvr + V7X_LANES]

    q_c = q_c * lax.rsqrt(jnp.mean(q_c * q_c, axis=-1, keepdims=True) + LN_EPS) * qg_ref[...]
    kv_c = kv_c * lax.rsqrt(jnp.mean(kv_c * kv_c, axis=-1, keepdims=True) + LN_EPS) * kvg_ref[...]

    kt = k_r * cs
    quarter = V7X_LANES // 4
    lane = lax.broadcasted_iota(jnp.int32, kt.shape, 1)
    swapped = jnp.where((lane // quarter) % 2 == 0,
                        pltpu.roll(kt, V7X_LANES - quarter, 1), pltpu.roll(kt, quarter, 1))
    k_ref[...] = jnp.concatenate([kv_c, kt + swapped], axis=-1).astype(BF16)
    vt_ref[0:kvr, :] = kv_c.T.astype(BF16)
    vt_ref[kvr:VT_ROWS, :] = jnp.ones((VT_ROWS - kvr, t), BF16)

    qn_t = (q_c * (ATTN_SCALE * LOG2E)).T.astype(BF16)

    def query_heads(g):
        per = MLA_HEADS // PROJ_PIECES
        qf_t = jnp.dot(wq_ref[g * per * HEAD_LANES:(g + 1) * per * HEAD_LANES, :], qn_t,
                       preferred_element_type=F32)
        for k in range(per):
            hh = g * per + k
            q_lat = qf_t[k * HEAD_LANES:k * HEAD_LANES + V7X_LANES]
            q_rope = qf_t[k * HEAD_LANES + V7X_LANES:(k + 1) * HEAD_LANES] * cs_t
            q_ref[hh] = jnp.concatenate([q_lat, q_rope], axis=0).astype(BF16)

    def gate_cols(g):
        zw = zg_ref.shape[1] // PROJ_PIECES
        z0 = qr + kvr + V7X_LANES + g * zw
        z = jnp.dot(h, w_in_ref[:, z0:z0 + zw], preferred_element_type=F32)
        zg_ref[:, g * zw:(g + 1) * zw] = _silu(z).astype(BF16)

    for g in range(PROJ_PIECES):
        gate_cols(g)
        query_heads(g)


def _layer1_proj(x, shift, scale, cos_t, sin_t, w_in, qg, kvg, wq):
    b, s, d = x.shape
    t = Q_TILE
    n = PROJ_STEP_TILES
    zw = w_in.shape[1] - (MLA_Q_RANK + MLA_KV_RANK + V7X_LANES)
    vec = pl.BlockSpec((1, 1, d), lambda i, j: (i, 0, 0))
    return pl.pallas_call(
        _l1_proj_kernel,
        grid=(b, s // (n * t)),
        in_specs=[
            pl.BlockSpec((1, n * t, d), lambda i, j: (i, j, 0)),
            vec, vec,
            pl.BlockSpec((1, MLA_ROPE // 2, n * t), lambda i, j: (i, 0, j)),
            pl.BlockSpec((1, MLA_ROPE // 2, n * t), lambda i, j: (i, 0, j)),
            _resident(w_in.shape),
            _resident((1, MLA_Q_RANK)), _resident((1, MLA_KV_RANK)),
            _resident(wq.shape),
        ],
        out_specs=[
            pl.BlockSpec((1, n, MLA_HEADS, HEAD_LANES, t), lambda i, j: (i, j, 0, 0, 0)),
            pl.BlockSpec((1, n * t, HEAD_LANES), lambda i, j: (i, j, 0)),
            pl.BlockSpec((1, VT_ROWS, n * t), lambda i, j: (i, 0, j)),
            pl.BlockSpec((1, n * t, zw), lambda i, j: (i, j, 0)),
        ],
        out_shape=[
            jax.ShapeDtypeStruct((b, s // t, MLA_HEADS, HEAD_LANES, t), BF16),
            jax.ShapeDtypeStruct((b, s, HEAD_LANES), BF16),
            jax.ShapeDtypeStruct((b, VT_ROWS, s), BF16),
            jax.ShapeDtypeStruct((b, s, zw), BF16),
        ],
        compiler_params=pltpu.CompilerParams(
            dimension_semantics=("arbitrary", "arbitrary"), vmem_limit_bytes=VMEM_LIMIT_BYTES),
        name="layer1_proj",
    )(x, shift, scale, cos_t, sin_t, w_in, qg.reshape(1, -1), kvg.reshape(1, -1), wq)


def _attn_kernel(q_ref, k_ref, vt_ref, zg_ref, x_ref, gate_ref, wuv_ref, w_out_ref, lng_ref, lnb_ref,
                 o_ref, s0_ref, s1_ref, mc0_ref, mc1_ref, m_ref, acc_ref, mix_ref):
    qt = x_ref.shape[1] // ATTN_STEP_TILES
    slot0, slot1 = (s0_ref, mc0_ref), (s1_ref, mc1_ref)
    assert ATTN_COL_GROUPS == MLA_HEADS and ATTN_STEP_TILES == 2

    for local in range(ATTN_STEP_TILES):
        qi = ATTN_STEP_TILES * pl.program_id(1) + local
        rows = slice(local * qt, (local + 1) * qt)

        def scores(start, s_ref, mc_ref, masked, g, local=local):
            start = pl.multiple_of(start, qt)
            st = jnp.dot(k_ref[0, pl.ds(start, qt), :], q_ref[0, local, g], preferred_element_type=F32)
            if masked:
                k_chunk = lax.broadcasted_iota(jnp.int32, (qt, qt), 0) // CHUNK
                q_chunk = lax.broadcasted_iota(jnp.int32, (qt, qt), 1) // CHUNK
                st = jnp.where(k_chunk <= q_chunk, st, MASK_VALUE)
            s_ref[g] = st
            mc_ref[g] = jnp.max(st, axis=0, keepdims=True)

        def accumulate(start, s_ref, mc_ref, g):
            start = pl.multiple_of(start, qt)
            m_prev = m_ref[g]
            m_new = jnp.maximum(m_prev, mc_ref[g])
            alpha = jnp.exp2(m_prev - m_new)
            pt = jnp.exp2(s_ref[g] - m_new).astype(BF16)
            pv = jnp.dot(vt_ref[0, :, pl.ds(start, qt)], pt, preferred_element_type=F32)
            acc_ref[g] = alpha * acc_ref[g] + pv
            m_ref[g] = m_new

        m_ref[...] = jnp.full(m_ref.shape, MASK_VALUE, F32)
        acc_ref[...] = jnp.zeros(acc_ref.shape, F32)

        def folded_start(i, qi=qi):
            return jnp.where(i == 0, qi, i - 1) * qt

        def trip(i, into, fold, scores=scores, accumulate=accumulate, folded_start=folded_start):
            for g in range(ATTN_COL_GROUPS):
                scores(i * qt, *into, False, g)
                accumulate(folded_start(i), *fold, g)

        odd = local % 2
        for g in range(ATTN_COL_GROUPS):
            scores(qi * qt, *(slot1 if odd else slot0), True, g)
        if odd:
            trip(0, slot0, slot1)

        def pair(p, carry, trip=trip, odd=odd):
            trip(2 * p + odd, slot1, slot0)
            trip(2 * p + odd + 1, slot0, slot1)
            return carry

        lax.fori_loop(0, pl.program_id(1), pair, 0)

        def gated_value(hh, rows=rows):
            inv_l = 1.0 / acc_ref[hh, MLA_KV_RANK:MLA_KV_RANK + 1, :]
            o_t = (acc_ref[hh, 0:MLA_KV_RANK, :] * inv_l).astype(BF16)
            o_h = lax.dot_general(o_t, wuv_ref[hh], (((0,), (0,)), ((), ())),
                                  preferred_element_type=F32)
            cols = slice(hh * MLA_V, (hh + 1) * MLA_V)
            mix_ref[:, cols] = (o_h * zg_ref[0, rows, cols].astype(F32)).astype(BF16)

        for hh in range(MLA_HEADS + EPILOGUE_LAG):
            if hh < MLA_HEADS:
                accumulate(folded_start(qi), *slot0, hh)
            if hh >= EPILOGUE_LAG:
                gated_value(hh - EPILOGUE_LAG)
        y = jnp.dot(mix_ref[...], w_out_ref[...], preferred_element_type=F32)
        res = DEEPNORM_ALPHA * x_ref[0, rows, :] + (1.0 + gate_ref[0]) * y
        o_ref[0, rows, :] = _layer_norm(res, lng_ref[...], lnb_ref[...])


def _layer1_attn(q, k, vt, zg, x, gate, wuv, w_out, lng, lnb):
    b, s, d = x.shape
    qt = Q_TILE
    n = ATTN_STEP_TILES
    width = zg.shape[2]
    return pl.pallas_call(
        _attn_kernel,
        grid=(b, s // (n * qt)),
        in_specs=[
            pl.BlockSpec((1, n, MLA_HEADS, HEAD_LANES, qt), lambda i, j: (i, j, 0, 0, 0)),
            pl.BlockSpec((1, s, HEAD_LANES), lambda i, j: (i, 0, 0)),
            pl.BlockSpec((1, VT_ROWS, s), lambda i, j: (i, 0, 0)),
            pl.BlockSpec((1, n * qt, width), lambda i, j: (i, j, 0)),
            pl.BlockSpec((1, n * qt, d), lambda i, j: (i, j, 0)),
            pl.BlockSpec((1, 1, d), lambda i, j: (i, 0, 0)),
            _resident(wuv.shape),
            _resident(w_out.shape),
            _resident((1, d)), _resident((1, d)),
        ],
        out_specs=pl.BlockSpec((1, n * qt, d), lambda i, j: (i, j, 0)),
        out_shape=jax.ShapeDtypeStruct((b, s, d), F32),
        scratch_shapes=[
            pltpu.VMEM((MLA_HEADS, qt, qt), F32),
            pltpu.VMEM((MLA_HEADS, qt, qt), F32),
            pltpu.VMEM((MLA_HEADS, 1, qt), F32),
            pltpu.VMEM((MLA_HEADS, 1, qt), F32),
            pltpu.VMEM((MLA_HEADS, 1, qt), F32),
            pltpu.VMEM((MLA_HEADS, VT_ROWS, qt), F32),
            pltpu.VMEM((qt, width), BF16),
        ],
        compiler_params=pltpu.CompilerParams(
            dimension_semantics=("arbitrary", "arbitrary"), vmem_limit_bytes=VMEM_LIMIT_BYTES),
        name="layer1_attn",
    )(q, k, vt, zg, x, gate, wuv, w_out, lng.reshape(1, -1), lnb.reshape(1, -1))


def _layer1_weights(w_in, w_uq, w_uk, w_uv):
    qr, kvr, half = MLA_Q_RANK, MLA_KV_RANK, MLA_ROPE // 2
    k1 = w_in[:, qr + kvr:qr + kvr + half]
    k2 = w_in[:, qr + kvr + half:qr + kvr + MLA_ROPE]
    w_in_p = jnp.concatenate(
        [w_in[:, :qr + kvr], k1, -k2, k2, k1, w_in[:, qr + kvr + MLA_ROPE:]], axis=1).astype(BF16)
    w_lat = jnp.transpose(_fold_query_key(w_uq, w_uk), (1, 0, 2))
    r1 = w_uq[:, :, MLA_NOPE:MLA_NOPE + half]
    r2 = w_uq[:, :, MLA_NOPE + half:]
    wq = jnp.concatenate([w_lat, r1, -r2, r2, r1], axis=2).reshape(qr, MLA_HEADS * HEAD_LANES).T.astype(BF16)
    wuv = jnp.transpose(w_uv, (1, 0, 2)).astype(BF16)
    return w_in_p, wq, wuv


def kernel(x, c, positions, ada_w, ada_b, ln_g, ln_b, e_w_in, gmlp_norm_g, gmlp_norm_b, gmlp_ws,
           gmlp_bs, pool_w, pool_b, pool_scale, e_w_out, o_w_in, mla_q_norm_g, mla_kv_norm_g,
           mla_w_uq, mla_w_uk, mla_w_uv, o_w_out):
    b, s, d = x.shape
    assert s % (L0_STEP_TILES * L0_TILE) == 0 and s % (PROJ_STEP_TILES * Q_TILE) == 0 and Q_TILE % CHUNK == 0
    assert s % (ATTN_STEP_TILES * Q_TILE) == 0
    assert L0_TILE % GMLP_BLOCK == 0 and L0_STEP_TILES % 2 == 0 and ada_w.shape[0] == DEPTH

    mod = _modulation(c, ada_w, ada_b)
    shift, scale, gate = (mod[:, :, i * d:(i + 1) * d].reshape(DEPTH, b, 1, d) for i in range(3))

    x = _layer0(x, shift[0], scale[0], gate[0], e_w_in[0], gmlp_norm_g[0], gmlp_norm_b[0], gmlp_ws[0],
                gmlp_bs[0], pool_w[0], pool_b[0], pool_scale[0], e_w_out[0], ln_g[0], ln_b[0])

    w_in_p, wq, wuv = _layer1_weights(o_w_in[0], mla_w_uq[0], mla_w_uk[0], mla_w_uv[0])
    cos_t, sin_t = _rope_tables(positions)
    q, k, vt, zg = _layer1_proj(x, shift[1], scale[1], cos_t, sin_t, w_in_p, mla_q_norm_g[0], mla_kv_norm_g[0], wq)
    return _layer1_attn(q, k, vt, zg, x, gate[1], wuv, o_w_out[0].astype(BF16), ln_g[1], ln_b[1])
```

```python
import functools
import math

import numpy as np
import jax
import jax.numpy as jnp
from jax import lax
from jax.experimental import pallas as pl
from jax.experimental.pallas import tpu as pltpu

F32 = jnp.float32
BF16 = jnp.bfloat16

CHUNK = 64
LN_EPS = 1e-5
GMLP_HEADS = 4
GMLP_BLOCK = 128
POOL_WINDOWS = (2, 4, 8, 16)
MLA_HEADS = 16
MLA_NOPE = 128
MLA_ROPE = 64
MLA_V = 128
MLA_Q_RANK = 256
MLA_KV_RANK = 128
ROPE_THETA = 10000.0
ATTN_SCALE = (MLA_NOPE + MLA_ROPE) ** -0.5
DEPTH = 2
DEEPNORM_ALPHA = (2.0 * DEPTH) ** 0.25

V7X_LANES = 128
V7X_MXU_WIDTH = 256
V7X_VMEM_BYTES = 64 * 1024 * 1024
VMEM_LIMIT_BYTES = V7X_VMEM_BYTES - 8 * 1024 * 1024

POOL_HALO = max(POOL_WINDOWS)
V7X_SUBLANES = 8
POOL_PAD = V7X_SUBLANES
HEAD_LANES = 2 * V7X_LANES
MASK_VALUE = -1e30
LOG2E = math.log2(math.e)

L0_TILE = 256
L0_STEP_TILES = 4
PROJ_STEP_TILES = 4
FOLD_STEP_HEADS = 4
Q_TILE = 256
ATTN_STEP_TILES = 2
ATTN_COL_GROUPS = MLA_HEADS
EPILOGUE_LAG = MLA_HEADS
PROJ_PIECES = 4
BF16_SUBLANES = 16
VT_ROWS = MLA_KV_RANK + BF16_SUBLANES


def _silu(v):
    half = 0.5 * v
    return half + half * jnp.tanh(half)


def _layer_norm(v, g, b):
    mu = jnp.mean(v, axis=-1, keepdims=True)
    d = v - mu
    var = jnp.mean(d * d, axis=-1, keepdims=True)
    return d * lax.rsqrt(var + LN_EPS) * g + b


def _resident(shape):
    nd = len(shape)
    return pl.BlockSpec(shape, lambda *_: (0,) * nd, pipeline_mode=pl.Buffered(1))


def _mod_kernel(c_ref, w_ref, b_ref, o_ref):
    cond = _silu(c_ref[...]).astype(BF16)
    o_ref[0] = jnp.dot(cond, w_ref[0].astype(BF16), preferred_element_type=F32) + b_ref[0]


def _modulation(c, ada_w, ada_b):
    depth, d, n = ada_w.shape
    b = c.shape[0]
    tn = d
    return pl.pallas_call(
        _mod_kernel,
        grid=(depth, n // tn),
        in_specs=[
            pl.BlockSpec((b, d), lambda l, j: (0, 0)),
            pl.BlockSpec((1, d, tn), lambda l, j: (l, 0, j)),
            pl.BlockSpec((1, 1, tn), lambda l, j: (l, 0, j)),
        ],
        out_specs=pl.BlockSpec((1, b, tn), lambda l, j: (l, 0, j)),
        out_shape=jax.ShapeDtypeStruct((depth, b, n), F32),
        compiler_params=pltpu.CompilerParams(
            dimension_semantics=("arbitrary", "arbitrary"), vmem_limit_bytes=VMEM_LIMIT_BYTES),
        name="adaln_mod",
    )(c, ada_w, ada_b.reshape(depth, 1, n))


def _l0_kernel(xp_ref, xn_ref, shc_ref, scc_ref, shn_ref, scn_ref, gate_ref, w_in_ref, ng_ref, nb_ref,
               ws_ref, bsf_ref, pw_ref, pb_ref, ps_ref, cnt_ref, w_out_ref, lng_ref, lnb_ref,
               o_ref, p0_ref, p1_ref, xbuf_ref, la_ref, lb_ref, mix_ref):
    t, d = xn_ref.shape[1], xn_ref.shape[2]
    width = d
    hd = width // GMLP_HEADS
    gd = width // len(POOL_WINDOWS)
    halo = POOL_PAD + POOL_HALO
    first_step = jnp.logical_and(pl.program_id(0) == 0, pl.program_id(1) == 0)
    seq_start = pl.program_id(1) == 0
    n_pieces = GMLP_HEADS + len(POOL_WINDOWS)
    mxu_cols = w_in_ref.shape[1] // V7X_MXU_WIDTH
    cuts = [V7X_MXU_WIDTH * ((k * mxu_cols) // n_pieces) for k in range(n_pieces + 1)]

    def modulate(x, shift, scale):
        return (x * (1.0 + scale) + shift).astype(BF16)

    def project_piece(h, p_ref, k):
        cols = slice(cuts[k], cuts[k + 1])
        p_ref[:, cols] = jnp.dot(h, w_in_ref[:, cols], preferred_element_type=F32)

    r = lax.broadcasted_iota(jnp.int32, (GMLP_BLOCK, GMLP_BLOCK), 0) // CHUNK
    c = lax.broadcasted_iota(jnp.int32, (GMLP_BLOCK, GMLP_BLOCK), 1) // CHUNK
    tri = c <= r

    def gmlp_head(p_ref, hh):
        cols = slice(hh * hd, (hh + 1) * hd)
        vn = _layer_norm(p_ref[:, width + hh * hd:width + (hh + 1) * hd],
                         ng_ref[...], nb_ref[...]).astype(BF16)
        w = jnp.where(tri, ws_ref[hh], 0.0).astype(BF16)
        for k in range(t // GMLP_BLOCK):
            rows = slice(k * GMLP_BLOCK, (k + 1) * GMLP_BLOCK)
            sv = jnp.dot(w, vn[rows], preferred_element_type=F32) + bsf_ref[:, cols]
            za = p_ref[rows, 2 * width + hh * hd:2 * width + (hh + 1) * hd]
            mix_ref[rows, cols] = (p_ref[rows, cols] * sv * _silu(za)).astype(BF16)

    def pool_fill(p_ref, is_seq_start):
        carry = xbuf_ref[POOL_PAD:halo, :]
        if is_seq_start is not None:
            carry = jnp.where(is_seq_start, 0.0, carry)
            xbuf_ref[POOL_PAD:halo, :] = carry
        xbuf_ref[halo:halo + t, :] = p_ref[:, 3 * width:4 * width]

    def pool_tail(p_ref, g, pooled, rows):
        cols = slice(g * gd, (g + 1) * gd)
        y = jnp.dot(pooled.astype(BF16), pw_ref[g], preferred_element_type=F32)
        zb = p_ref[rows, 4 * width + g * gd:4 * width + (g + 1) * gd]
        mix_ref[rows, width + g * gd:width + (g + 1) * gd] = (
            (y + pb_ref[:, cols]) * ps_ref[:, cols] * _silu(zb)).astype(BF16)

    def pool_group(p_ref, g, is_seq_start):
        win = POOL_WINDOWS[g]
        cols = slice(g * gd, (g + 1) * gd)
        levels = win.bit_length() - 1
        src = xbuf_ref.at[:, cols]
        for lvl in range(1, levels + 1):
            shift = 1 << (lvl - 1)
            lo = halo if lvl == levels else POOL_PAD
            n = halo + t - lo
            val = src[lo:lo + n, :] + src[lo - shift:lo - shift + n, :]
            if lvl < levels:
                dst = la_ref if lvl % 2 else lb_ref
                dst[lo:lo + n, :] = val
                src = dst
        tok = xbuf_ref[halo:halo + t, cols]
        pool_tail(p_ref, g, val * (1.0 / win) - tok, slice(0, t))
        if is_seq_start is not None:
            div = jnp.where(is_seq_start, cnt_ref[:, cols], float(win))
            pool_tail(p_ref, g, val[0:POOL_HALO] / div - tok[0:POOL_HALO], slice(0, POOL_HALO))

    def finish(x, out_rows):
        xbuf_ref[POOL_PAD:halo, :] = xbuf_ref[t + POOL_PAD:t + halo, :]
        y = jnp.dot(mix_ref[...], w_out_ref[...], preferred_element_type=F32)
        res = DEEPNORM_ALPHA * x + (1.0 + gate_ref[0]) * y
        o_ref[0, out_rows, :] = _layer_norm(res, lng_ref[...], lnb_ref[...])

    def half_step(h_next, p_into, p_from, x, out_rows, is_seq_start):
        pool_fill(p_from, is_seq_start)
        for k in range(n_pieces):
            project_piece(h_next, p_into, k)
            if k < GMLP_HEADS:
                gmlp_head(p_from, k)
            else:
                pool_group(p_from, k - GMLP_HEADS, is_seq_start)
        finish(x, out_rows)

    @pl.when(first_step)
    def _():
        for buf in (xbuf_ref, la_ref, lb_ref):
            buf[0:POOL_PAD, :] = jnp.zeros((POOL_PAD, buf.shape[1]), F32)
        h = modulate(xp_ref[0, 0:t, :], shc_ref[0], scc_ref[0])
        for k in range(n_pieces):
            project_piece(h, p0_ref, k)

    slots = (p0_ref, p1_ref)
    for i in range(L0_STEP_TILES):
        rows = slice(i * t, (i + 1) * t)
        if i + 1 < L0_STEP_TILES:
            h_next = modulate(xp_ref[0, (i + 1) * t:(i + 2) * t, :], shc_ref[0], scc_ref[0])
        else:
            h_next = modulate(xn_ref[0], shn_ref[0], scn_ref[0])
        half_step(h_next, slots[(i + 1) % 2], slots[i % 2], xp_ref[0, rows, :], rows,
                  seq_start if i == 0 else None)


def _layer0(x, shift, scale, gate, w_in, ng, nb, ws, bs, pw, pb, ps, w_out, lng, lnb):
    b, s, d = x.shape
    t = L0_TILE
    width = d
    hd = width // GMLP_HEADS
    gd = width // len(POOL_WINDOWS)
    bsf = jnp.broadcast_to(bs.T[:, :, None], (GMLP_BLOCK, GMLP_HEADS, hd)).reshape(GMLP_BLOCK, width)
    pos = np.arange(POOL_HALO, dtype=np.float32)[:, None] + 1.0
    cnt = np.concatenate(
        [np.broadcast_to(np.minimum(pos, float(w)), (POOL_HALO, gd)) for w in POOL_WINDOWS], axis=1)
    row = lambda a: a.reshape(1, -1)
    tiles = s // t
    per_step = L0_STEP_TILES

    def next_tile(i, j):
        flat = jnp.minimum(i * tiles + per_step * (j + 1), b * tiles - 1)
        return flat // tiles, flat % tiles

    vec = pl.BlockSpec((1, 1, d), lambda i, j: (i, 0, 0))
    vec_next = pl.BlockSpec((1, 1, d), lambda i, j: (next_tile(i, j)[0], 0, 0))
    return pl.pallas_call(
        _l0_kernel,
        grid=(b, tiles // per_step),
        in_specs=[
            pl.BlockSpec((1, per_step * t, d), lambda i, j: (i, j, 0)),
            pl.BlockSpec((1, t, d), lambda i, j: (*next_tile(i, j), 0)),
            vec, vec, vec_next, vec_next, vec,
            _resident(w_in.shape),
            _resident((1, hd)), _resident((1, hd)),
            _resident(ws.shape),
            _resident(bsf.shape),
            _resident(pw.shape),
            _resident((1, width)), _resident((1, width)),
            _resident(cnt.shape),
            _resident(w_out.shape),
            _resident((1, d)), _resident((1, d)),
        ],
        out_specs=pl.BlockSpec((1, per_step * t, d), lambda i, j: (i, j, 0)),
        out_shape=jax.ShapeDtypeStruct((b, s, d), F32),
        scratch_shapes=[
            pltpu.VMEM((t, w_in.shape[1]), F32),
            pltpu.VMEM((t, w_in.shape[1]), F32),
            pltpu.VMEM((POOL_PAD + POOL_HALO + t, width), F32),
            pltpu.VMEM((POOL_PAD + POOL_HALO + t, gd), F32),
            pltpu.VMEM((POOL_PAD + POOL_HALO + t, gd), F32),
            pltpu.VMEM((t, 2 * width), BF16),
        ],
        compiler_params=pltpu.CompilerParams(
            dimension_semantics=("arbitrary", "arbitrary"), vmem_limit_bytes=VMEM_LIMIT_BYTES),
        name="layer0_gmlp_pool",
    )(x, x, shift, scale, shift, scale, gate, w_in.astype(BF16), row(ng), row(nb), ws, bsf,
      pw.astype(BF16), row(pb), row(ps), jnp.asarray(cnt), w_out.astype(BF16), row(lng), row(lnb))


def _fold_kernel(wq_ref, wk_ref, o_ref):
    for hh in range(wq_ref.shape[0]):
        o_ref[hh] = lax.dot_general(wq_ref[hh], wk_ref[hh], (((1,), (1,)), ((), ())),
                                    precision=lax.Precision.HIGHEST, preferred_element_type=F32)


def _fold_query_key(w_uq, w_uk):
    qr = w_uq.shape[0]
    wq = jnp.transpose(w_uq[:, :, :MLA_NOPE], (1, 0, 2))
    wk = jnp.transpose(w_uk, (1, 0, 2))
    return pl.pallas_call(
        _fold_kernel,
        grid=(MLA_HEADS // FOLD_STEP_HEADS,),
        in_specs=[pl.BlockSpec((FOLD_STEP_HEADS, qr, MLA_NOPE), lambda h: (h, 0, 0)),
                  pl.BlockSpec((FOLD_STEP_HEADS, MLA_KV_RANK, MLA_NOPE), lambda h: (h, 0, 0))],
        out_specs=pl.BlockSpec((FOLD_STEP_HEADS, qr, MLA_KV_RANK), lambda h: (h, 0, 0)),
        out_shape=jax.ShapeDtypeStruct((MLA_HEADS, qr, MLA_KV_RANK), F32),
        name="fold_uq_uk",
    )(wq, wk)


def _rope_kernel(pos_ref, inv_ref, cos_ref, sin_ref):
    ang = inv_ref[...] * pos_ref[0].astype(F32)
    cos_ref[0] = jnp.cos(ang)
    sin_ref[0] = jnp.sin(ang)


def _rope_tables(positions):
    b, s = positions.shape
    half = MLA_ROPE // 2
    inv = (1.0 / (ROPE_THETA ** (np.arange(0, MLA_ROPE, 2, dtype=np.float64) / MLA_ROPE))).astype(np.float32)
    table = pl.BlockSpec((1, half, s), lambda i: (i, 0, 0))
    return pl.pallas_call(
        _rope_kernel,
        grid=(b,),
        in_specs=[pl.BlockSpec((1, 1, s), lambda i: (i, 0, 0)), _resident((half, 1))],
        out_specs=[table, table],
        out_shape=[jax.ShapeDtypeStruct((b, half, s), F32)] * 2,
        name="rope_tables",
    )(positions.reshape(b, 1, s), jnp.asarray(inv[:, None]))


def _l1_proj_kernel(x_ref, shift_ref, scale_ref, cos_ref, sin_ref, w_in_ref, qg_ref, kvg_ref, wq_ref,
                    q_ref, k_ref, vt_ref, zg_ref):
    t = x_ref.shape[1] // PROJ_STEP_TILES
    qr, kvr = MLA_Q_RANK, MLA_KV_RANK
    for i in range(PROJ_STEP_TILES):
        rows = slice(i * t, (i + 1) * t)
        _project_tile(x_ref[0, rows, :], shift_ref[0], scale_ref[0], cos_ref[0, :, rows], sin_ref[0, :, rows],
                      w_in_ref, qg_ref, kvg_ref, wq_ref, q_ref.at[0, i], k_ref.at[0, rows, :],
                      vt_ref.at[0, :, rows], zg_ref.at[0, rows, :], t, qr, kvr)


def _project_tile(x, shift, scale, cos_t, sin_t, w_in_ref, qg_ref, kvg_ref, wq_ref,
                  q_ref, k_ref, vt_ref, zg_ref, t, qr, kvr):
    h = (x * (1.0 + scale) + shift).astype(BF16)
    cs_t = jnp.concatenate([cos_t, sin_t, cos_t, sin_t], axis=0)
    cs = cs_t.T

    lat = jnp.dot(h, w_in_ref[:, 0:qr + kvr + V7X_LANES], preferred_element_type=F32)
    q_c = lat[:, 0:qr]
    kv_c = lat[:, qr:qr + kvr]
    k_r = lat[:, qr + kvr:qr + kvr + V7X_LANES]

    q_c = q_c * lax.rsqrt(jnp.mean(q_c * q_c, axis=-1, keepdims=True) + LN_EPS) * qg_ref[...]
    kv_c = kv_c * lax.rsqrt(jnp.mean(kv_c * kv_c, axis=-1, keepdims=True) + LN_EPS) * kvg_ref[...]

    kt = k_r * cs
    quarter = V7X_LANES // 4
    lane = lax.broadcasted_iota(jnp.int32, kt.shape, 1)
    swapped = jnp.where((lane // quarter) % 2 == 0,
                        pltpu.roll(kt, V7X_LANES - quarter, 1), pltpu.roll(kt, quarter, 1))
    k_ref[...] = jnp.concatenate([kv_c, kt + swapped], axis=-1).astype(BF16)
    vt_ref[0:kvr, :] = kv_c.T.astype(BF16)
    vt_ref[kvr:VT_ROWS, :] = jnp.ones((VT_ROWS - kvr, t), BF16)

    qn_t = (q_c * (ATTN_SCALE * LOG2E)).T.astype(BF16)

    def query_heads(g):
        per = MLA_HEADS // PROJ_PIECES
        qf_t = jnp.dot(wq_ref[g * per * HEAD_LANES:(g + 1) * per * HEAD_LANES, :], qn_t,
                       preferred_element_type=F32)
        for k in range(per):
            hh = g * per + k
            q_lat = qf_t[k * HEAD_LANES:k * HEAD_LANES + V7X_LANES]
            q_rope = qf_t[k * HEAD_LANES + V7X_LANES:(k + 1) * HEAD_LANES] * cs_t
            q_ref[hh] = jnp.concatenate([q_lat, q_rope], axis=0).astype(BF16)

    def gate_cols(g):
        zw = zg_ref.shape[1] // PROJ_PIECES
        z0 = qr + kvr + V7X_LANES + g * zw
        z = jnp.dot(h, w_in_ref[:, z0:z0 + zw], preferred_element_type=F32)
        zg_ref[:, g * zw:(g + 1) * zw] = _silu(z).astype(BF16)

    for g in range(PROJ_PIECES):
        gate_cols(g)
        query_heads(g)


def _layer1_proj(x, shift, scale, cos_t, sin_t, w_in, qg, kvg, wq):
    b, s, d = x.shape
    t = Q_TILE
    n = PROJ_STEP_TILES
    zw = w_in.shape[1] - (MLA_Q_RANK + MLA_KV_RANK + V7X_LANES)
    vec = pl.BlockSpec((1, 1, d), lambda i, j: (i, 0, 0))
    return pl.pallas_call(
        _l1_proj_kernel,
        grid=(b, s // (n * t)),
        in_specs=[
            pl.BlockSpec((1, n * t, d), lambda i, j: (i, j, 0)),
            vec, vec,
            pl.BlockSpec((1, MLA_ROPE // 2, n * t), lambda i, j: (i, 0, j)),
            pl.BlockSpec((1, MLA_ROPE // 2, n * t), lambda i, j: (i, 0, j)),
            _resident(w_in.shape),
            _resident((1, MLA_Q_RANK)), _resident((1, MLA_KV_RANK)),
            _resident(wq.shape),
        ],
        out_specs=[
            pl.BlockSpec((1, n, MLA_HEADS, HEAD_LANES, t), lambda i, j: (i, j, 0, 0, 0)),
            pl.BlockSpec((1, n * t, HEAD_LANES), lambda i, j: (i, j, 0)),
            pl.BlockSpec((1, VT_ROWS, n * t), lambda i, j: (i, 0, j)),
            pl.BlockSpec((1, n * t, zw), lambda i, j: (i, j, 0)),
        ],
        out_shape=[
            jax.ShapeDtypeStruct((b, s // t, MLA_HEADS, HEAD_LANES, t), BF16),
            jax.ShapeDtypeStruct((b, s, HEAD_LANES), BF16),
            jax.ShapeDtypeStruct((b, VT_ROWS, s), BF16),
            jax.ShapeDtypeStruct((b, s, zw), BF16),
        ],
        compiler_params=pltpu.CompilerParams(
            dimension_semantics=("arbitrary", "arbitrary"), vmem_limit_bytes=VMEM_LIMIT_BYTES),
        name="layer1_proj",
    )(x, shift, scale, cos_t, sin_t, w_in, qg.reshape(1, -1), kvg.reshape(1, -1), wq)


def _attn_kernel(q_ref, k_ref, vt_ref, zg_ref, x_ref, gate_ref, wuv_ref, w_out_ref, lng_ref, lnb_ref,
                 o_ref, s0_ref, s1_ref, mc0_ref, mc1_ref, m_ref, acc_ref, mix_ref):
    qt = x_ref.shape[1] // ATTN_STEP_TILES
    slot0, slot1 = (s0_ref, mc0_ref), (s1_ref, mc1_ref)
    assert ATTN_COL_GROUPS == MLA_HEADS and ATTN_STEP_TILES == 2

    for local in range(ATTN_STEP_TILES):
        qi = ATTN_STEP_TILES * pl.program_id(1) + local
        rows = slice(local * qt, (local + 1) * qt)

        def scores(start, s_ref, mc_ref, masked, g, local=local):
            start = pl.multiple_of(start, qt)
            st = jnp.dot(k_ref[0, pl.ds(start, qt), :], q_ref[0, local, g], preferred_element_type=F32)
            if masked:
                k_chunk = lax.broadcasted_iota(jnp.int32, (qt, qt), 0) // CHUNK
                q_chunk = lax.broadcasted_iota(jnp.int32, (qt, qt), 1) // CHUNK
                st = jnp.where(k_chunk <= q_chunk, st, MASK_VALUE)
            s_ref[g] = st
            mc_ref[g] = jnp.max(st, axis=0, keepdims=True)

        def accumulate(start, s_ref, mc_ref, g):
            start = pl.multiple_of(start, qt)
            m_prev = m_ref[g]
            m_new = jnp.maximum(m_prev, mc_ref[g])
            alpha = jnp.exp2(m_prev - m_new)
            pt = jnp.exp2(s_ref[g] - m_new).astype(BF16)
            pv = jnp.dot(vt_ref[0, :, pl.ds(start, qt)], pt, preferred_element_type=F32)
            acc_ref[g] = alpha * acc_ref[g] + pv
            m_ref[g] = m_new

        m_ref[...] = jnp.full(m_ref.shape, MASK_VALUE, F32)
        acc_ref[...] = jnp.zeros(acc_ref.shape, F32)

        def folded_start(i, qi=qi):
            return jnp.where(i == 0, qi, i - 1) * qt

        def trip(i, into, fold, scores=scores, accumulate=accumulate, folded_start=folded_start):
            for g in range(ATTN_COL_GROUPS):
                scores(i * qt, *into, False, g)
                accumulate(folded_start(i), *fold, g)

        odd = local % 2
        for g in range(ATTN_COL_GROUPS):
            scores(qi * qt, *(slot1 if odd else slot0), True, g)
        if odd:
            trip(0, slot0, slot1)

        def pair(p, carry, trip=trip, odd=odd):
            trip(2 * p + odd, slot1, slot0)
            trip(2 * p + odd + 1, slot0, slot1)
            return carry

        lax.fori_loop(0, pl.program_id(1), pair, 0)

        def gated_value(hh, rows=rows):
            inv_l = 1.0 / acc_ref[hh, MLA_KV_RANK:MLA_KV_RANK + 1, :]
            o_t = (acc_ref[hh, 0:MLA_KV_RANK, :] * inv_l).astype(BF16)
            o_h = lax.dot_general(o_t, wuv_ref[hh], (((0,), (0,)), ((), ())),
                                  preferred_element_type=F32)
            cols = slice(hh * MLA_V, (hh + 1) * MLA_V)
            mix_ref[:, cols] = (o_h * zg_ref[0, rows, cols].astype(F32)).astype(BF16)

        for hh in range(MLA_HEADS + EPILOGUE_LAG):
            if hh < MLA_HEADS:
                accumulate(folded_start(qi), *slot0, hh)
            if hh >= EPILOGUE_LAG:
                gated_value(hh - EPILOGUE_LAG)
        y = jnp.dot(mix_ref[...], w_out_ref[...], preferred_element_type=F32)
        res = DEEPNORM_ALPHA * x_ref[0, rows, :] + (1.0 + gate_ref[0]) * y
        o_ref[0, rows, :] = _layer_norm(res, lng_ref[...], lnb_ref[...])


def _layer1_attn(q, k, vt, zg, x, gate, wuv, w_out, lng, lnb):
    b, s, d = x.shape
    qt = Q_TILE
    n = ATTN_STEP_TILES
    width = zg.shape[2]
    return pl.pallas_call(
        _attn_kernel,
        grid=(b, s // (n * qt)),
        in_specs=[
            pl.BlockSpec((1, n, MLA_HEADS, HEAD_LANES, qt), lambda i, j: (i, j, 0, 0, 0)),
            pl.BlockSpec((1, s, HEAD_LANES), lambda i, j: (i, 0, 0)),
            pl.BlockSpec((1, VT_ROWS, s), lambda i, j: (i, 0, 0)),
            pl.BlockSpec((1, n * qt, width), lambda i, j: (i, j, 0)),
            pl.BlockSpec((1, n * qt, d), lambda i, j: (i, j, 0)),
            pl.BlockSpec((1, 1, d), lambda i, j: (i, 0, 0)),
            _resident(wuv.shape),
            _resident(w_out.shape),
            _resident((1, d)), _resident((1, d)),
        ],
        out_specs=pl.BlockSpec((1, n * qt, d), lambda i, j: (i, j, 0)),
        out_shape=jax.ShapeDtypeStruct((b, s, d), F32),
        scratch_shapes=[
            pltpu.VMEM((MLA_HEADS, qt, qt), F32),
            pltpu.VMEM((MLA_HEADS, qt, qt), F32),
            pltpu.VMEM((MLA_HEADS, 1, qt), F32),
            pltpu.VMEM((MLA_HEADS, 1, qt), F32),
            pltpu.VMEM((MLA_HEADS, 1, qt), F32),
            pltpu.VMEM((MLA_HEADS, VT_ROWS, qt), F32),
            pltpu.VMEM((qt, width), BF16),
        ],
        compiler_params=pltpu.CompilerParams(
            dimension_semantics=("arbitrary", "arbitrary"), vmem_limit_bytes=VMEM_LIMIT_BYTES),
        name="layer1_attn",
    )(q, k, vt, zg, x, gate, wuv, w_out, lng.reshape(1, -1), lnb.reshape(1, -1))


def _layer1_weights(w_in, w_uq, w_uk, w_uv):
    qr, kvr, half = MLA_Q_RANK, MLA_KV_RANK, MLA_ROPE // 2
    k1 = w_in[:, qr + kvr:qr + kvr + half]
    k2 = w_in[:, qr + kvr + half:qr + kvr + MLA_ROPE]
    w_in_p = jnp.concatenate(
        [w_in[:, :qr + kvr], k1, -k2, k2, k1, w_in[:, qr + kvr + MLA_ROPE:]], axis=1).astype(BF16)
    w_lat = jnp.transpose(_fold_query_key(w_uq, w_uk), (1, 0, 2))
    r1 = w_uq[:, :, MLA_NOPE:MLA_NOPE + half]
    r2 = w_uq[:, :, MLA_NOPE + half:]
    wq = jnp.concatenate([w_lat, r1, -r2, r2, r1], axis=2).reshape(qr, MLA_HEADS * HEAD_LANES).T.astype(BF16)
    wuv = jnp.transpose(w_uv, (1, 0, 2)).astype(BF16)
    return w_in_p, wq, wuv


def kernel(x, c, positions, ada_w, ada_b, ln_g, ln_b, e_w_in, gmlp_norm_g, gmlp_norm_b, gmlp_ws,
           gmlp_bs, pool_w, pool_b, pool_scale, e_w_out, o_w_in, mla_q_norm_g, mla_kv_norm_g,
           mla_w_uq, mla_w_uk, mla_w_uv, o_w_out):
    b, s, d = x.shape
    assert s % (L0_STEP_TILES * L0_TILE) == 0 and s % (PROJ_STEP_TILES * Q_TILE) == 0 and Q_TILE % CHUNK == 0
    assert s % (ATTN_STEP_TILES * Q_TILE) == 0
    assert L0_TILE % GMLP_BLOCK == 0 and L0_STEP_TILES % 2 == 0 and ada_w.shape[0] == DEPTH

    mod = _modulation(c, ada_w, ada_b)
    shift, scale, gate = (mod[:, :, i * d:(i + 1) * d].reshape(DEPTH, b, 1, d) for i in range(3))

    x = _layer0(x, shift[0], scale[0], gate[0], e_w_in[0], gmlp_norm_g[0], gmlp_norm_b[0], gmlp_ws[0],
                gmlp_bs[0], pool_w[0], pool_b[0], pool_scale[0], e_w_out[0], ln_g[0], ln_b[0])

    w_in_p, wq, wuv = _layer1_weights(o_w_in[0], mla_w_uq[0], mla_w_uk[0], mla_w_uv[0])
    cos_t, sin_t = _rope_tables(positions)
    q, k, vt, zg = _layer1_proj(x, shift[1], scale[1], cos_t, sin_t, w_in_p, mla_q_norm_g[0], mla_kv_norm_g[0], wq)
    return _layer1_attn(q, k, vt, zg, x, gate[1], wuv, o_w_out[0].astype(BF16), ln_g[1], ln_b[1])
```
